```python
import jax, jax.numpy as jnp
from jax import lax
import numpy as np

D_MODEL = 1024
BATCH = 4
SEQ = 4096
DEPTH = 4
DEC_BATCH = 8
DEC_SEQ = 64
PAST_LEN = 1024

CHUNK = 64
N_META = 16
N_MIXERS = 4
NORM_EPS = 1e-6
LN_EPS = 1e-5
CONV_W = 31
D_CONV = D_MODEL
D_RNN = 1280
LRU_BLOCKS = 10
LRU_BLOCK = D_RNN // LRU_BLOCKS
LRU_CONV_W = 4
LRU_C = 8.0
RWKV_HEAD = 64
RWKV_HEADS = D_MODEL // RWKV_HEAD
D_DECAY_LORA = 64
D_A_LORA = 64
RWKV_GN_EPS = 64e-5
RET_HEADS = 4
RET_DK = D_MODEL // RET_HEADS
RET_DV = 2 * RET_DK
D_RET_V = RET_HEADS * RET_DV
ROPE_BASE = 10000.0

kernel_name = 'hybrid_streaming_encoder_step'


def rmsnorm(x, g):
    x32 = x.astype(jnp.float32)
    y = x32 * lax.rsqrt(jnp.mean(x32 * x32, axis=-1, keepdims=True) + NORM_EPS)
    return (y * g.astype(jnp.float32)).astype(x.dtype)


def layernorm(x, g, b):
    x32 = x.astype(jnp.float32)
    mu = jnp.mean(x32, axis=-1, keepdims=True)
    var = jnp.mean(jnp.square(x32 - mu), axis=-1, keepdims=True)
    y = (x32 - mu) * lax.rsqrt(var + LN_EPS)
    return (y * g.astype(jnp.float32) + b.astype(jnp.float32)).astype(x.dtype)


def causal_dwconv(x, buf, w, b):
    width = w.shape[0]
    xp = jnp.concatenate([buf.astype(x.dtype), x], axis=1)
    y = lax.conv_general_dilated(xp, w[:, None, :].astype(x.dtype), window_strides=(1,), padding='VALID',
                                 dimension_numbers=('NWC', 'WIO', 'NWC'), feature_group_count=x.shape[-1])
    return y + b.astype(x.dtype), xp[:, xp.shape[1] - (width - 1):]


def conformer_conv_mixer(h, conv_buf, p):
    a, b, gate = jnp.split(h @ p['w_in'], 3, axis=-1)
    glu = a * jax.nn.sigmoid(b)
    c, new_buf = causal_dwconv(glu, conv_buf, p['w_dw'], p['b_dw'])
    c = layernorm(c, p['ln_g'], p['ln_b'])
    y = jax.nn.silu(c) * jax.nn.silu(gate)
    return y @ p['w_out'], new_buf


def _affine_combine(e, l):
    return (e[0] * l[0], l[0] * e[1] + l[1])


def rglru_mixer(h, conv_buf, h0, p):
    B, T, _ = h.shape
    xb, gate = jnp.split(h @ p['w_in'], 2, axis=-1)
    xc, new_buf = causal_dwconv(xb, conv_buf, p['w_conv'], p['b_conv'])
    xblk = xc.reshape(B, T, LRU_BLOCKS, LRU_BLOCK)
    r = jax.nn.sigmoid((jnp.einsum('btnk,nkj->btnj', xblk, p['w_rg']).reshape(B, T, D_RNN) + p['b_rg']).astype(jnp.float32))
    i = jax.nn.sigmoid((jnp.einsum('btnk,nkj->btnj', xblk, p['w_ig']).reshape(B, T, D_RNN) + p['b_ig']).astype(jnp.float32))
    log_a = -LRU_C * r * jax.nn.softplus(-p['lam'].astype(jnp.float32))
    a = jnp.exp(log_a)
    bx = jnp.sqrt(-jnp.expm1(2.0 * log_a)) * (i * xc.astype(jnp.float32))
    bx = bx.at[:, 0].add(a[:, 0] * h0.astype(jnp.float32))
    _, hs = lax.associative_scan(_affine_combine, (a, bx), axis=1)
    y = hs.astype(h.dtype) * jax.nn.silu(gate)
    return y @ p['w_out'], new_buf, hs[:, -1].astype(h.dtype)


def _wkv_step(S, inp):
    r_t, w_t, k_t, v_t, a_t, b_t = inp
    sa = jnp.einsum('bhij,bhj->bhi', S, a_t)
    S = S * w_t[:, :, None, :] + sa[..., None] * b_t[:, :, None, :] + v_t[..., None] * k_t[:, :, None, :]
    return S, jnp.einsum('bhij,bhj->bhi', S, r_t)


def rwkv7_mixer(h, shift_prev, S0, p):
    B, T, D = h.shape
    H, N = RWKV_HEADS, RWKV_HEAD
    h_prev = jnp.concatenate([shift_prev[:, None].astype(h.dtype), h[:, :-1]], axis=1)
    xx = h_prev - h
    xm = h[None] + xx[None] * p['mu'][:, None, None, :].astype(h.dtype)
    rkvg = jnp.einsum('sbtd,dse->sbte', xm[:4], p['w_in'].reshape(D, 4, D))
    r, k, v, g = rkvg[0], rkvg[1], rkvg[2], rkvg[3]
    w_log = -jax.nn.softplus(-(p['w0'] + jnp.tanh(xm[4] @ p['w1']) @ p['w2']).astype(jnp.float32)) - 0.5
    a = jax.nn.sigmoid((p['a0'] + (xm[5] @ p['a1']) @ p['a2']).astype(jnp.float32))

    def heads(z):
        return z.astype(jnp.float32).reshape(B, T, H, N)

    rh, vh, ah = heads(r), heads(v), heads(a)
    decay = jnp.exp(-jnp.exp(heads(w_log)))
    kk = heads(k * p['k_k'])
    kk = kk / jnp.maximum(jnp.sqrt(jnp.sum(kk * kk, axis=-1, keepdims=True)), 1e-12)
    kh = heads(k) * (1.0 + (ah - 1.0) * p['k_a'].astype(jnp.float32).reshape(H, N))
    aa = -kk
    bb = kk * ah
    xs = tuple(jnp.moveaxis(z, 1, 0) for z in (rh, decay, kh, vh, aa, bb))
    S_T, ys = lax.scan(_wkv_step, S0.astype(jnp.float32), xs)
    o = jnp.moveaxis(ys, 0, 1)
    mu = jnp.mean(o, axis=-1, keepdims=True)
    var = jnp.mean(jnp.square(o - mu), axis=-1, keepdims=True)
    o = ((o - mu) * lax.rsqrt(var + RWKV_GN_EPS)).reshape(B, T, D) * p['gn_g'].astype(jnp.float32) + p['gn_b'].astype(jnp.float32)
    bonus = jnp.sum(rh * kh * p['r_k'].astype(jnp.float32), axis=-1, keepdims=True) * vh
    o = (o + bonus.reshape(B, T, D)).astype(h.dtype) * jax.nn.silu(g)
    return o @ p['w_out'], h[:, -1], S_T.astype(h.dtype)


def rotary(x, pos):
    half = x.shape[-1] // 2
    inv = ROPE_BASE ** (-jnp.arange(half, dtype=jnp.float32) / half)
    ang = pos.astype(jnp.float32)[:, None] * inv[None]
    cos = jnp.cos(ang)[None, :, None, :]
    sin = jnp.sin(ang)[None, :, None, :]
    x1, x2 = x[..., :half], x[..., half:]
    return jnp.concatenate([x1 * cos - x2 * sin, x1 * sin + x2 * cos], axis=-1)


def retention_chunk(q, k, v, S, lg):
    L = q.shape[1]
    idx = jnp.arange(L, dtype=jnp.float32)
    diff = idx[:, None] - idx[None, :]
    causal = diff >= 0
    dmask = jnp.where(causal[None], jnp.exp(jnp.where(causal, diff, 0.0)[None] * lg[:, None, None]), 0.0)
    scores = jnp.einsum('bnhd,bmhd->bhnm', q, k) * dmask[None]
    o = jnp.einsum('bhnm,bmhe->bnhe', scores, v)
    o = o + jnp.einsum('bnhd,bhde->bnhe', q, S) * jnp.exp((idx[:, None] + 1.0) * lg[None, :])[None, :, :, None]
    k_dec = k * jnp.exp((L - 1.0 - idx)[:, None] * lg[None, :])[None, :, :, None]
    S = S * jnp.exp(L * lg)[None, :, None, None] + jnp.einsum('bmhd,bmhe->bhde', k_dec, v)
    return o, S


def retention_mixer(h, S0, pos, n_lead, p):
    B, T, _ = h.shape
    q, k, v, gate = jnp.split(h @ p['w_in'], [D_MODEL, 2 * D_MODEL, 2 * D_MODEL + D_RET_V], axis=-1)
    q = rotary(q.astype(jnp.float32).reshape(B, T, RET_HEADS, RET_DK), pos) * (RET_DK ** -0.5)
    k = rotary(k.astype(jnp.float32).reshape(B, T, RET_HEADS, RET_DK), pos)
    v = v.astype(jnp.float32).reshape(B, T, RET_HEADS, RET_DV)
    lg = jnp.log1p(-jnp.exp2(-5.0 - jnp.arange(RET_HEADS, dtype=jnp.float32)))
    S = S0.astype(jnp.float32)
    outs = []
    if n_lead > 0:
        o0, S = retention_chunk(q[:, :n_lead], k[:, :n_lead], v[:, :n_lead], S, lg)
        outs.append(o0)
    rest = T - n_lead
    if rest <= CHUNK:
        o1, S = retention_chunk(q[:, n_lead:], k[:, n_lead:], v[:, n_lead:], S, lg)
    else:
        nc = rest // CHUNK

        def blk(z):
            return jnp.moveaxis(z[:, n_lead:].reshape(B, nc, CHUNK, z.shape[2], z.shape[3]), 1, 0)

        def step(Sc, qkv):
            oc, Sc = retention_chunk(qkv[0], qkv[1], qkv[2], Sc, lg)
            return Sc, oc

        S, oc = lax.scan(step, S, (blk(q), blk(k), blk(v)))
        o1 = jnp.moveaxis(oc, 0, 1).reshape(B, rest, RET_HEADS, RET_DV)
    outs.append(o1)
    o = jnp.concatenate(outs, axis=1)
    o = o * lax.rsqrt(jnp.mean(o * o, axis=-1, keepdims=True) + NORM_EPS)
    o = o.reshape(B, T, D_RET_V) * p['gn_g'].astype(jnp.float32)
    y = o.astype(h.dtype) * jax.nn.silu(gate)
    return y @ p['w_out'], S.astype(h.dtype)


def run_trunk(x, st, pos, n_lead, norm_g, final_norm_g, pa, pb, pc, pd):
    new = {}
    for i in range(DEPTH):
        kind = i % N_MIXERS
        h = rmsnorm(x, norm_g[i])
        if kind == 0:
            y, new['conv_a'] = conformer_conv_mixer(h, st['conv_a'], pa)
        elif kind == 1:
            y, new['conv_b'], new['lru_b'] = rglru_mixer(h, st['conv_b'], st['lru_b'], pb)
        elif kind == 2:
            y, new['shift_c'], new['wkv_c'] = rwkv7_mixer(h, st['shift_c'], st['wkv_c'], pc)
        else:
            y, new['ret_d'] = retention_mixer(h, st['ret_d'], pos, n_lead, pd)
        x = x + y
    return rmsnorm(x, final_norm_g), new


def setup_inputs(seed: int = 0) -> dict:
    key = jax.random.key(seed)
    keys = iter(jax.random.split(key, 64))
    f32 = jnp.float32

    def nrm(shape, scale):
        return jax.random.normal(next(keys), shape, f32) * scale

    def unif(shape, lo, hi):
        return jax.random.uniform(next(keys), shape, f32, lo, hi)

    D = D_MODEL
    u = unif((D_RNN,), 0.9, 0.999)
    s = u ** (1.0 / LRU_C)
    lam = jnp.log(s) - jnp.log1p(-s)
    return {
        'x_prompt': nrm((BATCH, SEQ, D), 1.0),
        'x_sample': nrm((DEC_BATCH, DEC_SEQ, D), 1.0),
        'cache_conv_a': nrm((DEC_BATCH, CONV_W - 1, D_CONV), 0.5),
        'cache_conv_b': nrm((DEC_BATCH, LRU_CONV_W - 1, D_RNN), 1.0),
        'state_lru_b': nrm((DEC_BATCH, D_RNN), 0.5),
        'state_shift_c': nrm((DEC_BATCH, D), 1.0),
        'state_wkv_c': nrm((DEC_BATCH, RWKV_HEADS, RWKV_HEAD, RWKV_HEAD), 0.5),
        'state_ret_d': nrm((DEC_BATCH, RET_HEADS, RET_DK, RET_DV), 0.5),
        'meta_tokens': nrm((N_META, D), 1.0),
        'norm_g': 1.0 + nrm((DEPTH, D), 0.05),
        'final_norm_g': 1.0 + nrm((D,), 0.05),
        'a_w_in': nrm((D, 3 * D_CONV), D ** -0.5),
        'a_w_dw': nrm((CONV_W, D_CONV), CONV_W ** -0.5),
        'a_b_dw': nrm((D_CONV,), 0.02),
        'a_ln_g': 1.0 + nrm((D_CONV,), 0.05),
        'a_ln_b': nrm((D_CONV,), 0.02),
        'a_w_out': nrm((D_CONV, D), D_CONV ** -0.5),
        'b_w_in': nrm((D, 2 * D_RNN), D ** -0.5),
        'b_w_conv': nrm((LRU_CONV_W, D_RNN), LRU_CONV_W ** -0.5),
        'b_b_conv': nrm((D_RNN,), 0.02),
        'b_w_rg': nrm((LRU_BLOCKS, LRU_BLOCK, LRU_BLOCK), LRU_BLOCK ** -0.5),
        'b_b_rg': nrm((D_RNN,), 0.02),
        'b_w_ig': nrm((LRU_BLOCKS, LRU_BLOCK, LRU_BLOCK), LRU_BLOCK ** -0.5),
        'b_b_ig': nrm((D_RNN,), 0.02),
        'b_lam': lam,
        'b_w_out': nrm((D_RNN, D), D_RNN ** -0.5),
        'c_mu': unif((6, D), 0.0, 1.0),
        'c_w_in': nrm((D, 4 * D), D ** -0.5),
        'c_w0': unif((D,), -6.0, -1.0),
        'c_w1': nrm((D, D_DECAY_LORA), D ** -0.5),
        'c_w2': nrm((D_DECAY_LORA, D), 0.1),
        'c_a0': nrm((D,), 0.1),
        'c_a1': nrm((D, D_A_LORA), D ** -0.5),
        'c_a2': nrm((D_A_LORA, D), 0.5 * D_A_LORA ** -0.5),
        'c_k_k': 0.85 + nrm((D,), 0.05),
        'c_k_a': 1.0 + nrm((D,), 0.05),
        'c_r_k': nrm((RWKV_HEADS, RWKV_HEAD), 0.1),
        'c_gn_g': 1.0 + nrm((D,), 0.05),
        'c_gn_b': nrm((D,), 0.02),
        'c_w_out': nrm((D, D), D ** -0.5),
        'd_w_in': nrm((D, 2 * D + 2 * D_RET_V), D ** -0.5),
        'd_gn_g': 1.0 + nrm((D_RET_V,), 0.05),
        'd_w_out': nrm((D_RET_V, D), D_RET_V ** -0.5),
    }


def reference(x_prompt, x_sample, cache_conv_a, cache_conv_b, state_lru_b, state_shift_c, state_wkv_c, state_ret_d,
              meta_tokens, norm_g, final_norm_g,
              a_w_in, a_w_dw, a_b_dw, a_ln_g, a_ln_b, a_w_out,
              b_w_in, b_w_conv, b_b_conv, b_w_rg, b_b_rg, b_w_ig, b_b_ig, b_lam, b_w_out,
              c_mu, c_w_in, c_w0, c_w1, c_w2, c_a0, c_a1, c_a2, c_k_k, c_k_a, c_r_k, c_gn_g, c_gn_b, c_w_out,
              d_w_in, d_gn_g, d_w_out):
    pa = {'w_in': a_w_in, 'w_dw': a_w_dw, 'b_dw': a_b_dw, 'ln_g': a_ln_g, 'ln_b': a_ln_b, 'w_out': a_w_out}
    pb = {'w_in': b_w_in, 'w_conv': b_w_conv, 'b_conv': b_b_conv, 'w_rg': b_w_rg, 'b_rg': b_b_rg,
          'w_ig': b_w_ig, 'b_ig': b_b_ig, 'lam': b_lam, 'w_out': b_w_out}
    pc = {'mu': c_mu, 'w_in': c_w_in, 'w0': c_w0, 'w1': c_w1, 'w2': c_w2, 'a0': c_a0, 'a1': c_a1, 'a2': c_a2,
          'k_k': c_k_k, 'k_a': c_k_a, 'r_k': c_r_k, 'gn_g': c_gn_g, 'gn_b': c_gn_b, 'w_out': c_w_out}
    pd = {'w_in': d_w_in, 'gn_g': d_gn_g, 'w_out': d_w_out}

    B = x_prompt.shape[0]
    dt = x_prompt.dtype
    meta = jnp.broadcast_to(meta_tokens.astype(dt)[None], (B, N_META, D_MODEL))
    xp = jnp.concatenate([meta, x_prompt], axis=1)
    st0 = {
        'conv_a': jnp.zeros((B, CONV_W - 1, D_CONV), dt),
        'conv_b': jnp.zeros((B, LRU_CONV_W - 1, D_RNN), dt),
        'lru_b': jnp.zeros((B, D_RNN), dt),
        'shift_c': jnp.zeros((B, D_MODEL), dt),
        'wkv_c': jnp.zeros((B, RWKV_HEADS, RWKV_HEAD, RWKV_HEAD), dt),
        'ret_d': jnp.zeros((B, RET_HEADS, RET_DK, RET_DV), dt),
    }
    pos_p = jnp.arange(N_META + x_prompt.shape[1])
    yp, stp = run_trunk(xp, st0, pos_p, N_META, norm_g, final_norm_g, pa, pb, pc, pd)
    y_prompt = yp[:, N_META:]

    st_s = {'conv_a': cache_conv_a, 'conv_b': cache_conv_b, 'lru_b': state_lru_b,
            'shift_c': state_shift_c, 'wkv_c': state_wkv_c, 'ret_d': state_ret_d}
    pos_s = N_META + PAST_LEN + jnp.arange(x_sample.shape[1])
    y_sample, sts = run_trunk(x_sample, st_s, pos_s, 0, norm_g, final_norm_g, pa, pb, pc, pd)

    return (y_prompt, y_sample,
            stp['conv_a'], sts['conv_a'],
            stp['conv_b'], sts['conv_b'],
            stp['lru_b'], sts['lru_b'],
            stp['shift_c'], sts['shift_c'],
            stp['wkv_c'], sts['wkv_c'],
            stp['ret_d'], sts['ret_d'])
```

```python
import functools
import math

import jax
import jax.numpy as jnp
from jax import lax
from jax.experimental import pallas as pl
from jax.experimental.pallas import tpu as pltpu

F32 = jnp.float32
BF16 = jnp.bfloat16

D_MODEL = 1024
N_META = 16
PAST_LEN = 1024
NORM_EPS = 1e-6
LN_EPS = 1e-5
CONV_W = 31
HIST = 32

VMEM_LIMIT = 56 * 1024 * 1024


def _rms(x, g):
    return x * lax.rsqrt(jnp.mean(x * x, axis=-1, keepdims=True) + NORM_EPS) * g


def _silu(x):
    return x * jax.nn.sigmoid(x)


def _dot(a, b):
    return jnp.dot(a.astype(BF16), b, preferred_element_type=F32)


def _const_spec(shape):
    n = len(shape)
    return pl.BlockSpec(shape, lambda b, t: (0,) * n, pipeline_mode=pl.Buffered(1))


def _layer_a_kernel(x_ref, buf0_ref, g_ref, win_ref, wdw_ref, bdw_ref, lng_ref, lnb_ref, wout_ref,
                    xo_ref, bufo_ref, proj_ref, hist_ref, y_ref, *, tm):
    ti = pl.program_id(1)

    @pl.when(ti == 0)
    def _():
        hist_ref[0:HIST, :] = buf0_ref[0]

    x = x_ref[0]
    h = _rms(x, g_ref[...])
    proj_ref[...] = _dot(h, win_ref[...])
    hist_ref[HIST:HIST + tm, :] = proj_ref[:, 0:D_MODEL] * jax.nn.sigmoid(proj_ref[:, D_MODEL:2 * D_MODEL])

    rc = min(tm, 16)
    off = HIST - (CONV_W - 1)
    for r in range(tm // rc):
        acc = jnp.zeros((rc, D_MODEL), F32) + bdw_ref[...]
        for j in range(CONV_W):
            acc = acc + wdw_ref[j:j + 1, :] * hist_ref[r * rc + off + j:r * rc + off + j + rc, :]
        mu = jnp.mean(acc, axis=-1, keepdims=True)
        d = acc - mu
        var = jnp.mean(d * d, axis=-1, keepdims=True)
        c = d * lax.rsqrt(var + LN_EPS) * lng_ref[...] + lnb_ref[...]
        gate = proj_ref[r * rc:(r + 1) * rc, 2 * D_MODEL:3 * D_MODEL]
        y_ref[r * rc:(r + 1) * rc, :] = (_silu(c) * _silu(gate)).astype(BF16)

    xo_ref[0] = x + jnp.dot(y_ref[...], wout_ref[...], preferred_element_type=F32)

    if tm >= HIST:
        hist_ref[0:HIST, :] = hist_ref[tm:tm + HIST, :]
    else:
        hist_ref[0:HIST - tm, :] = hist_ref[tm:HIST, :]
        hist_ref[HIST - tm:HIST, :] = hist_ref[HIST:HIST + tm, :]
    bufo_ref[0] = hist_ref[0:HIST, :]


def _layer_a(x, buf, g, w_in, w_dw, b_dw, ln_g, ln_b, w_out, *, tm):
    B, T, D = x.shape
    buf32 = jnp.concatenate([jnp.zeros((B, HIST - (CONV_W - 1), D), F32), buf], axis=1)
    row = lambda v: v.reshape(1, -1)
    xo, bufo = pl.pallas_call(
        functools.partial(_layer_a_kernel, tm=tm),
        grid=(B, T // tm),
        in_specs=[
            pl.BlockSpec((1, tm, D), lambda b, t: (b, t, 0)),
            pl.BlockSpec((1, HIST, D), lambda b, t: (b, 0, 0)),
            _const_spec((1, D)),
            _const_spec((D, 3 * D)),
            _const_spec((CONV_W, D)),
            _const_spec((1, D)),
            _const_spec((1, D)),
            _const_spec((1, D)),
            _const_spec((D, D)),
        ],
        out_specs=[
            pl.BlockSpec((1, tm, D), lambda b, t: (b, t, 0)),
            pl.BlockSpec((1, HIST, D), lambda b, t: (b, 0, 0)),
        ],
        out_shape=[jax.ShapeDtypeStruct((B, T, D), F32), jax.ShapeDtypeStruct((B, HIST, D), F32)],
        scratch_shapes=[
            pltpu.VMEM((tm, 3 * D), F32),
            pltpu.VMEM((HIST + tm, D), F32),
            pltpu.VMEM((tm, D), BF16),
        ],
        compiler_params=pltpu.CompilerParams(
            dimension_semantics=("arbitrary", "arbitrary"), vmem_limit_bytes=VMEM_LIMIT),
        name="layer_a",
    )(x, buf32, row(g), w_in, w_dw, row(b_dw), row(ln_g), row(ln_b), w_out)
    return xo, bufo[:, HIST - (CONV_W - 1):]


D_RNN = 1280
LRU_BLOCKS = 10
LRU_BLOCK = 128
LRU_CONV_W = 4
LRU_C = 8.0
BHIST = 8


def _scan_affine_rows(a, b):
    n = a.shape[0]
    rows = lax.broadcasted_iota(jnp.int32, a.shape, 0)
    k = 1
    while k < n:
        a_s = jnp.where(rows >= k, pltpu.roll(a, k, 0), 1.0)
        b_s = jnp.where(rows >= k, pltpu.roll(b, k, 0), 0.0)
        b = a * b_s + b
        a = a * a_s
        k *= 2
    return a, b


def _one_minus_exp(z, ez):
    s = 1.0 + z * (1.0 / 9.0)
    for n in (8.0, 7.0, 6.0, 5.0, 4.0, 3.0, 2.0):
        s = 1.0 + (z * (1.0 / n)) * s
    return jnp.where(z > -0.125, -z * s, 1.0 - ez)


def _layer_b_kernel(x_ref, buf0_ref, h0_ref, g_ref, win_ref, wc_ref, bc_ref, wg_ref, brg_ref, big_ref, lam_ref,
                    wout_ref, xo_ref, bufo_ref, ho_ref, proj_ref, hist_ref, carry_ref, y_ref, *, tm):
    ti = pl.program_id(1)

    @pl.when(ti == 0)
    def _():
        hist_ref[0:BHIST, :] = buf0_ref[0]
        carry_ref[...] = h0_ref[0]

    x = x_ref[0]
    h = _rms(x, g_ref[...])
    proj_ref[...] = _dot(h, win_ref[...])
    hist_ref[BHIST:BHIST + tm, :] = proj_ref[:, 0:D_RNN]

    lam = lam_ref[...]
    neg_c_sp = -LRU_C * (jnp.maximum(-lam, 0.0) + jnp.log1p(jnp.exp(-jnp.abs(lam))))

    rc = min(tm, 64)
    off = BHIST - (LRU_CONV_W - 1)
    for r in range(tm // rc):
        r0 = r * rc
        xc = jnp.zeros((rc, D_RNN), F32) + bc_ref[...]
        for j in range(LRU_CONV_W):
            xc = xc + wc_ref[j:j + 1, :] * hist_ref[r0 + off + j:r0 + off + j + rc, :]
        xcb = xc.astype(BF16)
        rg, ig = [], []
        for n in range(LRU_BLOCKS):
            gg = jnp.dot(xcb[:, n * LRU_BLOCK:(n + 1) * LRU_BLOCK], wg_ref[n], preferred_element_type=F32)
            rg.append(gg[:, 0:LRU_BLOCK])
            ig.append(gg[:, LRU_BLOCK:2 * LRU_BLOCK])
        rgate = jax.nn.sigmoid(jnp.concatenate(rg, axis=1) + brg_ref[...])
        igate = jax.nn.sigmoid(jnp.concatenate(ig, axis=1) + big_ref[...])
        log_a = neg_c_sp * rgate
        a = jnp.exp(log_a)
        bx = jnp.sqrt(_one_minus_exp(2.0 * log_a, a * a)) * (igate * xc)
        a_cum, b_cum = _scan_affine_rows(a, bx)
        hs = a_cum * carry_ref[...] + b_cum
        carry_ref[...] = hs[rc - 1:rc, :]
        gate = proj_ref[r0:r0 + rc, D_RNN:2 * D_RNN]
        y_ref[r0:r0 + rc, :] = (hs * _silu(gate)).astype(BF16)

    xo_ref[0] = x + jnp.dot(y_ref[...], wout_ref[...], preferred_element_type=F32)
    hist_ref[0:BHIST, :] = hist_ref[tm:tm + BHIST, :]
    bufo_ref[0] = hist_ref[0:BHIST, :]
    ho_ref[0] = carry_ref[...]


def _layer_b(x, buf, h0, g, w_in, w_conv, b_conv, w_g, b_rg, b_ig, lam, w_out, *, tm):
    B, T, D = x.shape
    buf8 = jnp.concatenate([jnp.zeros((B, BHIST - (LRU_CONV_W - 1), D_RNN), F32), buf], axis=1)
    row = lambda v: v.reshape(1, -1)
    xo, bufo, ho = pl.pallas_call(
        functools.partial(_layer_b_kernel, tm=tm),
        grid=(B, T // tm),
        in_specs=[
            pl.BlockSpec((1, tm, D), lambda b, t: (b, t, 0)),
            pl.BlockSpec((1, BHIST, D_RNN), lambda b, t: (b, 0, 0)),
            pl.BlockSpec((1, 1, D_RNN), lambda b, t: (b, 0, 0)),
            _const_spec((1, D)),
            _const_spec((D, 2 * D_RNN)),
            _const_spec((LRU_CONV_W, D_RNN)),
            _const_spec((1, D_RNN)),
            _const_spec((LRU_BLOCKS, LRU_BLOCK, 2 * LRU_BLOCK)),
            _const_spec((1, D_RNN)),
            _const_spec((1, D_RNN)),
            _const_spec((1, D_RNN)),
            _const_spec((D_RNN, D)),
        ],
        out_specs=[
            pl.BlockSpec((1, tm, D), lambda b, t: (b, t, 0)),
            pl.BlockSpec((1, BHIST, D_RNN), lambda b, t: (b, 0, 0)),
            pl.BlockSpec((1, 1, D_RNN), lambda b, t: (b, 0, 0)),
        ],
        out_shape=[jax.ShapeDtypeStruct((B, T, D), F32), jax.ShapeDtypeStruct((B, BHIST, D_RNN), F32),
                   jax.ShapeDtypeStruct((B, 1, D_RNN), F32)],
        scratch_shapes=[
            pltpu.VMEM((tm, 2 * D_RNN), F32),
            pltpu.VMEM((BHIST + tm, D_RNN), F32),
            pltpu.VMEM((1, D_RNN), F32),
            pltpu.VMEM((tm, D_RNN), BF16),
        ],
        compiler_params=pltpu.CompilerParams(
            dimension_semantics=("arbitrary", "arbitrary"), vmem_limit_bytes=VMEM_LIMIT),
        name="layer_b",
    )(x, buf8, h0.reshape(B, 1, D_RNN), row(g), w_in, w_conv, row(b_conv), w_g, row(b_rg), row(b_ig), row(lam), w_out)
    return xo, bufo[:, BHIST - (LRU_CONV_W - 1):], ho.reshape(B, D_RNN)


RET_HEADS = 4
RET_DK = 256
RET_DV = 512
D_RET_V = RET_HEADS * RET_DV
ROPE_BASE = 10000.0


def _rot(z, cos, sin):
    half = RET_DK // 2
    z1, z2 = z[:, 0:half], z[:, half:RET_DK]
    return jnp.concatenate([z1 * cos - z2 * sin, z1 * sin + z2 * cos], axis=1)


def _layer_d_kernel(x_ref, s0_ref, cos_ref, sin_ref, g_ref, win_ref, gng_ref, wout_ref, fg_ref,
                    xo_ref, so_ref, proj_ref, y_ref, *, tm, lgs):
    ti = pl.program_id(1)

    @pl.when(ti == 0)
    def _():
        so_ref[...] = s0_ref[...]

    x = x_ref[0]
    h = _rms(x, g_ref[...])
    proj_ref[...] = _dot(h, win_ref[...])
    cos = cos_ref[...]
    sin = sin_ref[...]
    row = lax.broadcasted_iota(jnp.int32, (tm, tm), 0)
    col = lax.broadcasted_iota(jnp.int32, (tm, tm), 1)
    causal = row >= col
    diff = jnp.where(causal, row - col, 0).astype(F32)
    idx = lax.broadcasted_iota(jnp.int32, (tm, 1), 0).astype(F32)

    for hh in range(RET_HEADS):
        lg = lgs[hh]
        q = _rot(proj_ref[:, hh * RET_DK:(hh + 1) * RET_DK], cos, sin) * (RET_DK ** -0.5)
        k = _rot(proj_ref[:, D_MODEL + hh * RET_DK:D_MODEL + (hh + 1) * RET_DK], cos, sin)
        v = proj_ref[:, 2 * D_MODEL + hh * RET_DV:2 * D_MODEL + (hh + 1) * RET_DV].astype(BF16)
        qb = q.astype(BF16)
        dmask = jnp.where(causal, jnp.exp(diff * lg), 0.0)
        scores = lax.dot_general(qb, k.astype(BF16), (((1,), (1,)), ((), ())), preferred_element_type=F32) * dmask
        s_old = so_ref[0, hh]
        o = jnp.dot(scores.astype(BF16), v, preferred_element_type=F32)
        o = o + jnp.dot(qb, s_old.astype(BF16), preferred_element_type=F32) * jnp.exp((idx + 1.0) * lg)
        k_dec = (k * jnp.exp((tm - 1.0 - idx) * lg)).astype(BF16)
        so_ref[0, hh] = s_old * jnp.exp(jnp.full((1, 1), tm * lg, F32)) + lax.dot_general(
            k_dec, v, (((0,), (0,)), ((), ())), preferred_element_type=F32)
        o = o * lax.rsqrt(jnp.mean(o * o, axis=-1, keepdims=True) + NORM_EPS)
        o = o * gng_ref[:, hh * RET_DV:(hh + 1) * RET_DV]
        gate = proj_ref[:, 2 * D_MODEL + D_RET_V + hh * RET_DV:2 * D_MODEL + D_RET_V + (hh + 1) * RET_DV]
        y_ref[:, hh * RET_DV:(hh + 1) * RET_DV] = (o * _silu(gate)).astype(BF16)

    xn = x + jnp.dot(y_ref[...], wout_ref[...], preferred_element_type=F32)
    xo_ref[0] = _rms(xn, fg_ref[...])


def _layer_d(x, s0, pos0, g, w_in, gn_g, w_out, final_g, *, tm):
    B, T, D = x.shape
    lgs =tuple(math.log1p(-(2.0 ** (-5.0 - hh))) for hh in range(RET_HEADS))
    half = RET_DK // 2
    inv = ROPE_BASE ** (-jnp.arange(half, dtype=F32) / half)
    ang = (pos0 + jnp.arange(T)).astype(F32)[:, None] * inv[None]
    cos, sin = jnp.cos(ang), jnp.sin(ang)
    row = lambda v: v.reshape(1, -1)
    nproj = 2 * D + 2 * D_RET_V
    xo, so = pl.pallas_call(
        functools.partial(_layer_d_kernel, tm=tm, lgs=lgs),
        grid=(B, T // tm),
        in_specs=[
            pl.BlockSpec((1, tm, D), lambda b, t: (b, t, 0)),
            pl.BlockSpec((1, RET_HEADS, RET_DK, RET_DV), lambda b, t: (b, 0, 0, 0)),
            pl.BlockSpec((tm, half), lambda b, t: (t, 0)),
            pl.BlockSpec((tm, half), lambda b, t: (t, 0)),
            _const_spec((1, D)),
            _const_spec((D, nproj)),
            _const_spec((1, D_RET_V)),
            _const_spec((D_RET_V, D)),
            _const_spec((1, D)),
        ],
        out_specs=[
            pl.BlockSpec((1, tm, D), lambda b, t: (b, t, 0)),
            pl.BlockSpec((1, RET_HEADS, RET_DK, RET_DV), lambda b, t: (b, 0, 0, 0)),
        ],
        out_shape=[jax.ShapeDtypeStruct((B, T, D), F32),
                   jax.ShapeDtypeStruct((B, RET_HEADS, RET_DK, RET_DV), F32)],
        scratch_shapes=[
            pltpu.VMEM((tm, nproj), F32),
            pltpu.VMEM((tm, D_RET_V), BF16),
        ],
        compiler_params=pltpu.CompilerParams(
            dimension_semantics=("arbitrary", "arbitrary"), vmem_limit_bytes=VMEM_LIMIT),
        name="layer_d",
    )(x, s0, cos, sin, row(g), w_in, row(gn_g), w_out, row(final_g))
    return xo, so


RWKV_HEAD = 64
RWKV_HEADS = 16
D_LORA = 64
RWKV_GN_EPS = 64e-5
LANES = 128


def _seg_mats():
    c = jnp.arange(D_MODEL)[:, None] // RWKV_HEAD
    hcol = jnp.arange(LANES)[None, :]
    seg = (c == hcol).astype(BF16)
    return seg, seg.T


def _segsum(z, seg):
    return jnp.dot(z.astype(BF16), seg, preferred_element_type=F32)


def _segbcast(s, bc):
    hi = s.astype(BF16)
    lo = (s - hi.astype(F32)).astype(BF16)
    return jnp.dot(hi, bc, preferred_element_type=F32) + jnp.dot(lo, bc, preferred_element_type=F32)


def _layer_c1_kernel(x_ref, sh0_ref, g_ref, mu_ref, win_ref, w0_ref, w1_ref, w2_ref, a0_ref, a1_ref, a2_ref,
                     kk_ref, ka_ref, rk_ref, seg_ref, bc_ref,
                     r_ref, w_ref, k_ref, v_ref, a_ref, b_ref, gt_ref, bonus_ref, sho_ref, carry_ref, *, tm):
    ti = pl.program_id(1)

    @pl.when(ti == 0)
    def _():
        carry_ref[...] = sh0_ref[0]

    h = _rms(x_ref[0], g_ref[...])
    rows = lax.broadcasted_iota(jnp.int32, h.shape, 0)
    h_prev = jnp.where(rows == 0, carry_ref[...], pltpu.roll(h, 1, 0))
    carry_ref[...] = h[tm - 1:tm, :]
    sho_ref[0] = h[tm - 1:tm, :]
    xx = h_prev - h

    def mix(s):
        return (h + xx * mu_ref[s:s + 1, :]).astype(BF16)

    r = jnp.dot(mix(0), win_ref[:, 0:D_MODEL], preferred_element_type=F32)
    k = jnp.dot(mix(1), win_ref[:, D_MODEL:2 * D_MODEL], preferred_element_type=F32)
    v = jnp.dot(mix(2), win_ref[:, 2 * D_MODEL:3 * D_MODEL], preferred_element_type=F32)
    gt_ref[0] = jnp.dot(mix(3), win_ref[:, 3 * D_MODEL:4 * D_MODEL], preferred_element_type=F32)
    r_ref[0] = r
    v_ref[0] = v

    wl = jnp.tanh(jnp.dot(mix(4), w1_ref[...], preferred_element_type=F32))
    wx = w0_ref[...] + _dot(wl, w2_ref[...])
    w_log = -(jnp.maximum(-wx, 0.0) + jnp.log1p(jnp.exp(-jnp.abs(wx)))) - 0.5
    w_ref[0] = jnp.exp(-jnp.exp(w_log))

    al = jnp.dot(mix(5), a1_ref[...], preferred_element_type=F32)
    a = jax.nn.sigmoid(a0_ref[...] + _dot(al, a2_ref[...]))

    kk = k * kk_ref[...]
    n2 = _segsum(kk * kk, seg_ref[...])
    inv = 1.0 / jnp.maximum(jnp.sqrt(n2), 1e-12)
    kk = kk * _segbcast(inv, bc_ref[...])
    kh = k * (1.0 + (a - 1.0) * ka_ref[...])
    k_ref[0] = kh
    a_ref[0] = -kk
    b_ref[0] = kk * a
    rk = _segsum(r * kh * rk_ref[...], seg_ref[...])
    bonus_ref[0] = _segbcast(rk, bc_ref[...]) * v


def _layer_c1(x, sh0, g, mu, w_in, w0, w1, w2, a0, a1, a2, k_k, k_a, r_k, *, tm):
    B, T, D = x.shape
    seg, bc = _seg_mats()
    row = lambda v: v.reshape(1, -1)
    tile = pl.BlockSpec((1, tm, D), lambda b, t: (b, t, 0))
    outs = pl.pallas_call(
        functools.partial(_layer_c1_kernel, tm=tm),
        grid=(B, T // tm),
        in_specs=[
            tile,
            pl.BlockSpec((1, 1, D), lambda b, t: (b, 0, 0)),
            _const_spec((1, D)),
            _const_spec((6, D)),
            _const_spec((D, 4 * D)),
            _const_spec((1, D)),
            _const_spec((D, D_LORA)),
            _const_spec((D_LORA, D)),
            _const_spec((1, D)),
            _const_spec((D, D_LORA)),
            _const_spec((D_LORA, D)),
            _const_spec((1, D)),
            _const_spec((1, D)),
            _const_spec((1, D)),
            _const_spec((D, LANES)),
            _const_spec((LANES, D)),
        ],
        out_specs=[tile] * 8 + [pl.BlockSpec((1, 1, D), lambda b, t: (b, 0, 0))],
        out_shape=[jax.ShapeDtypeStruct((B, T, D), F32)] * 8 + [jax.ShapeDtypeStruct((B, 1, D), F32)],
        scratch_shapes=[pltpu.VMEM((1, D), F32)],
        compiler_params=pltpu.CompilerParams(
            dimension_semantics=("arbitrary", "arbitrary"), vmem_limit_bytes=VMEM_LIMIT),
        name="layer_c1",
    )(x, sh0.reshape(B, 1, D), row(g), mu, w_in, row(w0), w1, w2, row(a0), a1, a2, row(k_k), row(k_a), row(r_k),
      seg, bc)
    return outs[:8], outs[8].reshape(B, D)


def _layer_c3_kernel(x_ref, o_ref, bonus_ref, gt_ref, gng_ref, gnb_ref, wout_ref, seg_ref, bc_ref, xo_ref):
    o = o_ref[0]
    mu = _segsum(o, seg_ref[...]) * (1.0 / RWKV_HEAD)
    d = o - _segbcast(mu, bc_ref[...])
    var = _segsum(d * d, seg_ref[...]) * (1.0 / RWKV_HEAD)
    o = d * _segbcast(lax.rsqrt(var + RWKV_GN_EPS), bc_ref[...]) * gng_ref[...] + gnb_ref[...]
    y = (o + bonus_ref[0]) * _silu(gt_ref[0])
    xo_ref[0] = x_ref[0] + _dot(y, wout_ref[...])


def _layer_c3(x, o, bonus, gt, gn_g, gn_b, w_out, *, tm):
    B, T, D = x.shape
    seg, bc = _seg_mats()
    row = lambda v: v.reshape(1, -1)
    tile = pl.BlockSpec((1, tm, D), lambda b, t: (b, t, 0))
    return pl.pallas_call(
        _layer_c3_kernel,
        grid=(B, T // tm),
        in_specs=[tile, tile, tile, tile, _const_spec((1, D)), _const_spec((1, D)), _const_spec((D, D)),
                  _const_spec((D, LANES)), _const_spec((LANES, D))],
        out_specs=tile,
        out_shape=jax.ShapeDtypeStruct((B, T, D), F32),
        compiler_params=pltpu.CompilerParams(
            dimension_semantics=("arbitrary", "arbitrary"), vmem_limit_bytes=VMEM_LIMIT),
        name="layer_c3",
    )(x, o, bonus, gt, row(gn_g), row(gn_b), w_out, seg, bc)


def _wkv_kernel(r_ref, w_ref, k_ref, a_ref, b_ref, v_ref, s0_ref, y_ref, so_ref, col_ref, *, tq, ni, packed):
    @pl.when(pl.program_id(0) == 0)
    def _():
        so_ref[...] = s0_ref[...]

    lane = lax.broadcasted_iota(jnp.int32, (RWKV_HEAD, LANES), 1)
    low = lane < (LANES // 2)
    srcs = (r_ref, w_ref, k_ref, a_ref, b_ref)

    def one_token(t):
        for i in range(ni):
            s = so_ref[i]
            sa = jnp.sum(s * col_ref[3], axis=0, keepdims=True)
            s = s * col_ref[1] + sa * col_ref[4] + v_ref[t, i:i + 1, :] * col_ref[2]
            so_ref[i] = s
            y_ref[t, i:i + 1, :] = jnp.sum(s * col_ref[0], axis=0, keepdims=True)

    if packed:
        def body(tp, carry):
            for par in range(2):
                for q in range(5):
                    xq = srcs[q][tp]
                    rolled = pltpu.roll(xq, LANES // 2, 1)
                    col_ref[q] = jnp.where(low, xq, rolled) if par == 0 else jnp.where(low, rolled, xq)
                one_token(2 * tp + par)
            return carry
        lax.fori_loop(0, tq // 2, body, 0)
    else:
        def body(t, carry):
            for q in range(5):
                col_ref[q] = srcs[q][t]
            one_token(t)
            return carry
        lax.fori_loop(0, tq, body, 0)


def _wkv(cols, v, s0, *, tq, ni, packed):
    T = v.shape[0]
    tc = tq // 2 if packed else tq
    col_spec = pl.BlockSpec((tc, RWKV_HEAD, LANES), lambda t: (t, 0, 0))
    row_spec = pl.BlockSpec((tq, ni, LANES), lambda t: (t, 0, 0))
    st_spec = pl.BlockSpec((ni, RWKV_HEAD, LANES), lambda t: (0, 0, 0))
    return pl.pallas_call(
        functools.partial(_wkv_kernel, tq=tq, ni=ni, packed=packed),
        grid=(T // tq,),
        in_specs=[col_spec] * 5 + [row_spec, st_spec],
        out_specs=[row_spec, st_spec],
        out_shape=[jax.ShapeDtypeStruct((T, ni, LANES), F32), jax.ShapeDtypeStruct((ni, RWKV_HEAD, LANES), F32)],
        scratch_shapes=[pltpu.VMEM((5, RWKV_HEAD, LANES), F32)],
        compiler_params=pltpu.CompilerParams(dimension_semantics=("arbitrary",), vmem_limit_bytes=VMEM_LIMIT),
        name="wkv",
    )(*cols, v, s0)


def _wkv_layer(r, w, k, v, a, b, s0):
    B, T, D = r.shape
    H, N = RWKV_HEADS, RWKV_HEAD
    P = B * H
    packed = P == LANES // 2
    assert packed or P == LANES
    if packed:
        ni = N // 2
        to_col = lambda z: z.reshape(B, T // 2, 2, H, N).transpose(1, 4, 2, 0, 3).reshape(T // 2, N, LANES)
        to_row = lambda z: z.reshape(B, T, H, 2, ni).transpose(1, 4, 3, 0, 2).reshape(T, ni, LANES)
        from_row = lambda y: y.reshape(T, ni, 2, B, H).transpose(3, 0, 4, 2, 1).reshape(B, T, D)
        st_in = s0.reshape(B, H, 2, ni, N).transpose(3, 4, 2, 0, 1).reshape(ni, N, LANES)
        st_out = lambda s: s.reshape(ni, N, 2, B, H).transpose(3, 4, 2, 0, 1).reshape(B, H, N, N)
    else:
        ni = N
        to_col = lambda z: z.reshape(B, T, H, N).transpose(1, 3, 0, 2).reshape(T, N, LANES)
        to_row = to_col
        from_row = lambda y: y.reshape(T, N, B, H).transpose(2, 0, 3, 1).reshape(B, T, D)
        st_in = s0.transpose(2, 3, 0, 1).reshape(N, N, LANES)
        st_out = lambda s: s.reshape(N, N, B, H).transpose(2, 3, 0, 1)
    tq = min(T, 64)
    y, s = _wkv([to_col(z) for z in (r, w, k, a, b)], to_row(v), st_in, tq=tq, ni=ni, packed=packed)
    return from_row(y), st_out(s)


def kernel(x_prompt, x_sample, cache_conv_a, cache_conv_b, state_lru_b, state_shift_c, state_wkv_c, state_ret_d, meta_tokens, norm_g, final_norm_g, a_w_in, a_w_dw, a_b_dw, a_ln_g, a_ln_b, a_w_out, b_w_in, b_w_conv, b_b_conv, b_w_rg, b_b_rg, b_w_ig, b_b_ig, b_lam, b_w_out, c_mu, c_w_in, c_w0, c_w1, c_w2, c_a0, c_a1, c_a2, c_k_k, c_k_a, c_r_k, c_gn_g, c_gn_b, c_w_out, d_w_in, d_gn_g, d_w_out):
    bf = lambda z: z.astype(BF16)
    a_win, a_wout = bf(a_w_in), bf(a_w_out)
    b_win, b_wout = bf(b_w_in), bf(b_w_out)
    b_wg = bf(jnp.concatenate([b_w_rg, b_w_ig], axis=-1))
    c_win, c_wout = bf(c_w_in), bf(c_w_out)
    c_w1b, c_w2b, c_a1b, c_a2b = bf(c_w1), bf(c_w2), bf(c_a1), bf(c_a2)
    d_win, d_wout = bf(d_w_in), bf(d_w_out)

    def trunk(x, st, pos0, tm):
        x, conv_a = _layer_a(x, st[0], norm_g[0], a_win, a_w_dw, a_b_dw, a_ln_g, a_ln_b, a_wout, tm=tm)
        x, conv_b, lru = _layer_b(x, st[1], st[2], norm_g[1], b_win, b_w_conv, b_b_conv, b_wg, b_b_rg, b_b_ig,
                                  b_lam, b_wout, tm=tm)
        (r, w, k, v, a, b, gt, bonus), shift = _layer_c1(
            x, st[3], norm_g[2], c_mu, c_win, c_w0, c_w1b, c_w2b, c_a0, c_a1b, c_a2b, c_k_k, c_k_a,
            c_r_k.reshape(-1), tm=tm)
        o, wkv = _wkv_layer(r, w, k, v, a, b, st[4])
        x = _layer_c3(x, o, bonus, gt, c_gn_g, c_gn_b, c_wout, tm=tm)
        x, ret = _layer_d(x, st[5], pos0, norm_g[3], d_win, d_gn_g, d_wout, final_norm_g, tm=tm)
        return x, (conv_a, conv_b, lru, shift, wkv, ret)

    B = x_prompt.shape[0]
    zeros = (jnp.zeros((B, CONV_W - 1, D_MODEL), F32), jnp.zeros((B, LRU_CONV_W - 1, D_RNN), F32),
             jnp.zeros((B, D_RNN), F32), jnp.zeros((B, D_MODEL), F32),
             jnp.zeros((B, RWKV_HEADS, RWKV_HEAD, RWKV_HEAD), F32), jnp.zeros((B, RET_HEADS, RET_DK, RET_DV), F32))
    meta = jnp.broadcast_to(meta_tokens[None], (B, N_META, D_MODEL))
    _, st_meta = trunk(meta, zeros, 0, N_META)
    y_prompt, stp = trunk(x_prompt, st_meta, N_META, 256)
    st_s = (cache_conv_a, cache_conv_b, state_lru_b, state_shift_c, state_wkv_c, state_ret_d)
    y_sample, sts = trunk(x_sample, st_s, N_META + PAST_LEN, x_sample.shape[1])
    return (y_prompt, y_sample, stp[0], sts[0], stp[1], sts[1], stp[2], sts[2], stp[3], sts[3],
            stp[4], sts[4], stp[5], sts[5])
```

```python
import functools
import math

import jax
import jax.numpy as jnp
from jax import lax
from jax.experimental import pallas as pl
from jax.experimental.pallas import tpu as pltpu

F32 = jnp.float32
BF16 = jnp.bfloat16

D_MODEL = 1024
N_META = 16
PAST_LEN = 1024
NORM_EPS = 1e-6
LN_EPS = 1e-5
CONV_W = 31
HIST = 32
LANES = 128
SUBLANES = 8

VMEM_LIMIT = 56 * 1024 * 1024


def _rms(x, g):
    return x * lax.rsqrt(jnp.mean(x * x, axis=-1, keepdims=True) + NORM_EPS) * g


def _silu(x):
    return x * jax.nn.sigmoid(x)


def _dot(a, b):
    return jnp.dot(a.astype(BF16), b, preferred_element_type=F32)


def _dwconv_block(hist_ref, w_ref, b_ref, r0, rc, ls, width, nhist):
    off = nhist - (width - 1)
    nwin = rc + nhist
    win = hist_ref[r0:r0 + nwin, ls]
    acc = jnp.zeros((rc, ls.stop - ls.start), F32) + b_ref[:, ls]
    for res in range(SUBLANES):
        taps = [s for s in range(res, nhist + 1, SUBLANES) if 0 <= s - off < width]
        if not taps:
            continue
        wr = win if res == 0 else pltpu.roll(win, nwin - res, 0)
        for s in taps:
            acc = acc + w_ref[s - off:s - off + 1, ls] * wr[s - res:s - res + rc]
    return acc


def _const_spec(shape):
    n = len(shape)
    return pl.BlockSpec(shape, lambda b, t: (0,) * n, pipeline_mode=pl.Buffered(1))


def _layer_a_kernel(x_ref, buf0_ref, g_ref, win_ref, wdw_ref, bdw_ref, lng_ref, lnb_ref, wout_ref,
                    xo_ref, bufo_ref, proj_ref, hist_ref, c_ref, y_ref, *, tm):
    ti = pl.program_id(1)

    @pl.when(ti == 0)
    def _():
        hist_ref[0:HIST, :] = buf0_ref[0]

    x = x_ref[0]
    h = _rms(x, g_ref[...])
    proj_ref[...] = _dot(h, win_ref[...])
    hist_ref[HIST:HIST + tm, :] = proj_ref[:, 0:D_MODEL] * jax.nn.sigmoid(proj_ref[:, D_MODEL:2 * D_MODEL])

    rc = min(tm, 64)
    for r in range(tm // rc):
        for lb in range(D_MODEL // LANES):
            ls = slice(lb * LANES, (lb + 1) * LANES)
            c_ref[r * rc:(r + 1) * rc, ls] = _dwconv_block(hist_ref, wdw_ref, bdw_ref, r * rc, rc, ls, CONV_W, HIST)

    rc = min(tm, 16)
    for r in range(tm // rc):
        acc = c_ref[r * rc:(r + 1) * rc, :]
        mu = jnp.mean(acc, axis=-1, keepdims=True)
        d = acc - mu
        var = jnp.mean(d * d, axis=-1, keepdims=True)
        c = d * lax.rsqrt(var + LN_EPS) * lng_ref[...] + lnb_ref[...]
        gate = proj_ref[r * rc:(r + 1) * rc, 2 * D_MODEL:3 * D_MODEL]
        y_ref[r * rc:(r + 1) * rc, :] = (_silu(c) * _silu(gate)).astype(BF16)

    xo_ref[0] = x + jnp.dot(y_ref[...], wout_ref[...], preferred_element_type=F32)

    if tm >= HIST:
        hist_ref[0:HIST, :] = hist_ref[tm:tm + HIST, :]
    else:
        hist_ref[0:HIST - tm, :] = hist_ref[tm:HIST, :]
        hist_ref[HIST - tm:HIST, :] = hist_ref[HIST:HIST + tm, :]
    bufo_ref[0] = hist_ref[0:HIST, :]


def _layer_a(x, buf, g, w_in, w_dw, b_dw, ln_g, ln_b, w_out, *, tm):
    B, T, D = x.shape
    buf32 = jnp.concatenate([jnp.zeros((B, HIST - (CONV_W - 1), D), F32), buf], axis=1)
    row = lambda v: v.reshape(1, -1)
    xo, bufo = pl.pallas_call(
        functools.partial(_layer_a_kernel, tm=tm),
        grid=(B, T // tm),
        in_specs=[
            pl.BlockSpec((1, tm, D), lambda b, t: (b, t, 0)),
            pl.BlockSpec((1, HIST, D), lambda b, t: (b, 0, 0)),
            _const_spec((1, D)),
            _const_spec((D, 3 * D)),
            _const_spec((CONV_W, D)),
            _const_spec((1, D)),
            _const_spec((1, D)),
            _const_spec((1, D)),
            _const_spec((D, D)),
        ],
        out_specs=[
            pl.BlockSpec((1, tm, D), lambda b, t: (b, t, 0)),
            pl.BlockSpec((1, HIST, D), lambda b, t: (b, 0, 0)),
        ],
        out_shape=[jax.ShapeDtypeStruct((B, T, D), F32), jax.ShapeDtypeStruct((B, HIST, D), F32)],
        scratch_shapes=[
            pltpu.VMEM((tm, 3 * D), F32),
            pltpu.VMEM((HIST + tm, D), F32),
            pltpu.VMEM((tm, D), F32),
            pltpu.VMEM((tm, D), BF16),
        ],
        compiler_params=pltpu.CompilerParams(
            dimension_semantics=("arbitrary", "arbitrary"), vmem_limit_bytes=VMEM_LIMIT),
        name="layer_a",
    )(x, buf32, row(g), w_in, w_dw, row(b_dw), row(ln_g), row(ln_b), w_out)
    return xo, bufo[:, HIST - (CONV_W - 1):]


D_RNN = 1280
LRU_BLOCKS = 10
LRU_BLOCK = 128
LRU_CONV_W = 4
LRU_C = 8.0
BHIST = 8


def _scan_affine_rows(a, b):
    n = a.shape[0]
    rows = lax.broadcasted_iota(jnp.int32, a.shape, 0)
    k = 1
    while k < n:
        a_s = jnp.where(rows >= k, pltpu.roll(a, k, 0), 1.0)
        b_s = jnp.where(rows >= k, pltpu.roll(b, k, 0), 0.0)
        b = a * b_s + b
        a = a * a_s
        k *= 2
    return a, b


def _one_minus_exp(z, ez):
    s = 1.0 / 120.0
    for c in (1.0 / 24.0, 1.0 / 6.0, 0.5, 1.0):
        s = c + z * s
    return jnp.where(z > -1.0 / 32.0, -z * s, 1.0 - ez)


def _sqrt_nonneg(x):
    return jnp.where(x > 0.0, x * lax.rsqrt(x), 0.0)


def _layer_b_kernel(x_ref, buf0_ref, h0_ref, g_ref, win_ref, wc_ref, bc_ref, wg_ref, brg_ref, big_ref, lam_ref,
                    wout_ref, xo_ref, bufo_ref, ho_ref, proj_ref, hist_ref, carry_ref, y_ref, *, tm):
    ti = pl.program_id(1)

    @pl.when(ti == 0)
    def _():
        hist_ref[0:BHIST, :] = buf0_ref[0]
        carry_ref[...] = h0_ref[0]

    x = x_ref[0]
    h = _rms(x, g_ref[...])
    proj_ref[...] = _dot(h, win_ref[...])
    hist_ref[BHIST:BHIST + tm, :] = proj_ref[:, 0:D_RNN]

    lam = lam_ref[...]
    neg_c_sp = -LRU_C * (jnp.maximum(-lam, 0.0) + jnp.log1p(jnp.exp(-jnp.abs(lam))))

    rc = min(tm, 64)
    for r in range(tm // rc):
        r0 = r * rc
        xc = jnp.concatenate(
            [_dwconv_block(hist_ref, wc_ref, bc_ref, r0, rc, slice(lb * LANES, (lb + 1) * LANES), LRU_CONV_W, BHIST)
             for lb in range(D_RNN // LANES)], axis=1)
        xcb = xc.astype(BF16)
        rg, ig = [], []
        for n in range(LRU_BLOCKS):
            gg = jnp.dot(xcb[:, n * LRU_BLOCK:(n + 1) * LRU_BLOCK], wg_ref[n], preferred_element_type=F32)
            rg.append(gg[:, 0:LRU_BLOCK])
            ig.append(gg[:, LRU_BLOCK:2 * LRU_BLOCK])
        rgate = jax.nn.sigmoid(jnp.concatenate(rg, axis=1) + brg_ref[...])
        igate = jax.nn.sigmoid(jnp.concatenate(ig, axis=1) + big_ref[...])
        log_a = neg_c_sp * rgate
        a = jnp.exp(log_a)
        bx = _sqrt_nonneg(_one_minus_exp(2.0 * log_a, a * a)) * (igate * xc)
        a_cum, b_cum = _scan_affine_rows(a, bx)
        hs = a_cum * carry_ref[...] + b_cum
        carry_ref[...] = hs[rc - 1:rc, :]
        gate = proj_ref[r0:r0 + rc, D_RNN:2 * D_RNN]
        y_ref[r0:r0 + rc, :] = (hs * _silu(gate)).astype(BF16)

    xo_ref[0] = x + jnp.dot(y_ref[...], wout_ref[...], preferred_element_type=F32)
    hist_ref[0:BHIST, :] = hist_ref[tm:tm + BHIST, :]
    bufo_ref[0] = hist_ref[0:BHIST, :]
    ho_ref[0] = carry_ref[...]


def _layer_b(x, buf, h0, g, w_in, w_conv, b_conv, w_g, b_rg, b_ig, lam, w_out, *, tm):
    B, T, D = x.shape
    buf8 = jnp.concatenate([jnp.zeros((B, BHIST - (LRU_CONV_W - 1), D_RNN), F32), buf], axis=1)
    row = lambda v: v.reshape(1, -1)
    xo, bufo, ho = pl.pallas_call(
        functools.partial(_layer_b_kernel, tm=tm),
        grid=(B, T // tm),
        in_specs=[
            pl.BlockSpec((1, tm, D), lambda b, t: (b, t, 0)),
            pl.BlockSpec((1, BHIST, D_RNN), lambda b, t: (b, 0, 0)),
            pl.BlockSpec((1, 1, D_RNN), lambda b, t: (b, 0, 0)),
            _const_spec((1, D)),
            _const_spec((D, 2 * D_RNN)),
            _const_spec((LRU_CONV_W, D_RNN)),
            _const_spec((1, D_RNN)),
            _const_spec((LRU_BLOCKS, LRU_BLOCK, 2 * LRU_BLOCK)),
            _const_spec((1, D_RNN)),
            _const_spec((1, D_RNN)),
            _const_spec((1, D_RNN)),
            _const_spec((D_RNN, D)),
        ],
        out_specs=[
            pl.BlockSpec((1, tm, D), lambda b, t: (b, t, 0)),
            pl.BlockSpec((1, BHIST, D_RNN), lambda b, t: (b, 0, 0)),
            pl.BlockSpec((1, 1, D_RNN), lambda b, t: (b, 0, 0)),
        ],
        out_shape=[jax.ShapeDtypeStruct((B, T, D), F32), jax.ShapeDtypeStruct((B, BHIST, D_RNN), F32),
                   jax.ShapeDtypeStruct((B, 1, D_RNN), F32)],
        scratch_shapes=[
            pltpu.VMEM((tm, 2 * D_RNN), F32),
            pltpu.VMEM((BHIST + tm, D_RNN), F32),
            pltpu.VMEM((1, D_RNN), F32),
            pltpu.VMEM((tm, D_RNN), BF16),
        ],
        compiler_params=pltpu.CompilerParams(
            dimension_semantics=("arbitrary", "arbitrary"), vmem_limit_bytes=VMEM_LIMIT),
        name="layer_b",
    )(x, buf8, h0.reshape(B, 1, D_RNN), row(g), w_in, w_conv, row(b_conv), w_g, row(b_rg), row(b_ig), row(lam), w_out)
    return xo, bufo[:, BHIST - (LRU_CONV_W - 1):], ho.reshape(B, D_RNN)


RET_HEADS = 4
RET_DK = 256
RET_DV = 512
D_RET_V = RET_HEADS * RET_DV
ROPE_BASE = 10000.0


def _rot(z, cos, sin):
    half = RET_DK // 2
    z1, z2 = z[:, 0:half], z[:, half:RET_DK]
    return jnp.concatenate([z1 * cos - z2 * sin, z1 * sin + z2 * cos], axis=1)


def _layer_d_kernel(x_ref, s0_ref, cos_ref, sin_ref, g_ref, win_ref, gng_ref, wout_ref, fg_ref,
                    xo_ref, so_ref, proj_ref, y_ref, *, tm, lgs):
    ti = pl.program_id(1)

    @pl.when(ti == 0)
    def _():
        so_ref[...] = s0_ref[...]

    x = x_ref[0]
    h = _rms(x, g_ref[...])
    proj_ref[...] = _dot(h, win_ref[...])
    cos = cos_ref[...]
    sin = sin_ref[...]
    row = lax.broadcasted_iota(jnp.int32, (tm, tm), 0)
    col = lax.broadcasted_iota(jnp.int32, (tm, tm), 1)
    causal = row >= col
    diff = jnp.where(causal, row - col, 0).astype(F32)
    idx = lax.broadcasted_iota(jnp.int32, (tm, 1), 0).astype(F32)

    for hh in range(RET_HEADS):
        lg = lgs[hh]
        q = _rot(proj_ref[:, hh * RET_DK:(hh + 1) * RET_DK], cos, sin) * (RET_DK ** -0.5)
        k = _rot(proj_ref[:, D_MODEL + hh * RET_DK:D_MODEL + (hh + 1) * RET_DK], cos, sin)
        v = proj_ref[:, 2 * D_MODEL + hh * RET_DV:2 * D_MODEL + (hh + 1) * RET_DV].astype(BF16)
        qb = q.astype(BF16)
        dmask = jnp.where(causal, jnp.exp(diff * lg), 0.0)
        scores = lax.dot_general(qb, k.astype(BF16), (((1,), (1,)), ((), ())), preferred_element_type=F32) * dmask
        s_old = so_ref[0, hh]
        o = jnp.dot(scores.astype(BF16), v, preferred_element_type=F32)
        o = o + jnp.dot(qb, s_old.astype(BF16), preferred_element_type=F32) * jnp.exp((idx + 1.0) * lg)
        k_dec = (k * jnp.exp((tm - 1.0 - idx) * lg)).astype(BF16)
        so_ref[0, hh] = s_old * jnp.exp(jnp.full((1, 1), tm * lg, F32)) + lax.dot_general(
            k_dec, v, (((0,), (0,)), ((), ())), preferred_element_type=F32)
        o = o * lax.rsqrt(jnp.mean(o * o, axis=-1, keepdims=True) + NORM_EPS)
        o = o * gng_ref[:, hh * RET_DV:(hh + 1) * RET_DV]
        gate = proj_ref[:, 2 * D_MODEL + D_RET_V + hh * RET_DV:2 * D_MODEL + D_RET_V + (hh + 1) * RET_DV]
        y_ref[:, hh * RET_DV:(hh + 1) * RET_DV] = (o * _silu(gate)).astype(BF16)

    xn = x + jnp.dot(y_ref[...], wout_ref[...], preferred_element_type=F32)
    xo_ref[0] = _rms(xn, fg_ref[...])


def _layer_d(x, s0, pos0, g, w_in, gn_g, w_out, final_g, *, tm):
    B, T, D = x.shape
    lgs =tuple(math.log1p(-(2.0 ** (-5.0 - hh))) for hh in range(RET_HEADS))
    half = RET_DK // 2
    inv = ROPE_BASE ** (-jnp.arange(half, dtype=F32) / half)
    ang = (pos0 + jnp.arange(T)).astype(F32)[:, None] * inv[None]
    cos, sin = jnp.cos(ang), jnp.sin(ang)
    row = lambda v: v.reshape(1, -1)
    nproj = 2 * D + 2 * D_RET_V
    xo, so = pl.pallas_call(
        functools.partial(_layer_d_kernel, tm=tm, lgs=lgs),
        grid=(B, T // tm),
        in_specs=[
            pl.BlockSpec((1, tm, D), lambda b, t: (b, t, 0)),
            pl.BlockSpec((1, RET_HEADS, RET_DK, RET_DV), lambda b, t: (b, 0, 0, 0)),
            pl.BlockSpec((tm, half), lambda b, t: (t, 0)),
            pl.BlockSpec((tm, half), lambda b, t: (t, 0)),
            _const_spec((1, D)),
            _const_spec((D, nproj)),
            _const_spec((1, D_RET_V)),
            _const_spec((D_RET_V, D)),
            _const_spec((1, D)),
        ],
        out_specs=[
            pl.BlockSpec((1, tm, D), lambda b, t: (b, t, 0)),
            pl.BlockSpec((1, RET_HEADS, RET_DK, RET_DV), lambda b, t: (b, 0, 0, 0)),
        ],
        out_shape=[jax.ShapeDtypeStruct((B, T, D), F32),
                   jax.ShapeDtypeStruct((B, RET_HEADS, RET_DK, RET_DV), F32)],
        scratch_shapes=[
            pltpu.VMEM((tm, nproj), F32),
            pltpu.VMEM((tm, D_RET_V), BF16),
        ],
        compiler_params=pltpu.CompilerParams(
            dimension_semantics=("arbitrary", "arbitrary"), vmem_limit_bytes=VMEM_LIMIT),
        name="layer_d",
    )(x, s0, cos, sin, row(g), w_in, row(gn_g), w_out, row(final_g))
    return xo, so


RWKV_HEAD = 64
RWKV_HEADS = 16
D_LORA = 64
RWKV_GN_EPS = 64e-5


def _seg_mats():
    c = jnp.arange(D_MODEL)[:, None] // RWKV_HEAD
    hcol = jnp.arange(LANES)[None, :]
    seg = (c == hcol).astype(BF16)
    return seg, seg.T


def _segsum(z, seg):
    return jnp.dot(z.astype(BF16), seg, preferred_element_type=F32)


def _segbcast(s, bc):
    hi = s.astype(BF16)
    lo = (s - hi.astype(F32)).astype(BF16)
    return jnp.dot(hi, bc, preferred_element_type=F32) + jnp.dot(lo, bc, preferred_element_type=F32)


def _layer_c1_kernel(x_ref, sh0_ref, g_ref, mu_ref, win_ref, w0_ref, w1_ref, w2_ref, a0_ref, a1_ref, a2_ref,
                     kk_ref, ka_ref, rk_ref, seg_ref, bc_ref,
                     r_ref, w_ref, k_ref, v_ref, a_ref, b_ref, gt_ref, bonus_ref, sho_ref, carry_ref, *, tm):
    ti = pl.program_id(1)

    @pl.when(ti == 0)
    def _():
        carry_ref[...] = sh0_ref[0]

    h = _rms(x_ref[0], g_ref[...])
    rows = lax.broadcasted_iota(jnp.int32, h.shape, 0)
    h_prev = jnp.where(rows == 0, carry_ref[...], pltpu.roll(h, 1, 0))
    carry_ref[...] = h[tm - 1:tm, :]
    sho_ref[0] = h[tm - 1:tm, :]
    xx = h_prev - h

    def mix(s):
        return (h + xx * mu_ref[s:s + 1, :]).astype(BF16)

    r = jnp.dot(mix(0), win_ref[:, 0:D_MODEL], preferred_element_type=F32)
    k = jnp.dot(mix(1), win_ref[:, D_MODEL:2 * D_MODEL], preferred_element_type=F32)
    v = jnp.dot(mix(2), win_ref[:, 2 * D_MODEL:3 * D_MODEL], preferred_element_type=F32)
    gt_ref[0] = jnp.dot(mix(3), win_ref[:, 3 * D_MODEL:4 * D_MODEL], preferred_element_type=F32)
    r_ref[0] = r
    v_ref[0] = v

    wl = jnp.tanh(jnp.dot(mix(4), w1_ref[...], preferred_element_type=F32))
    wx = w0_ref[...] + _dot(wl, w2_ref[...])
    w_log = -(jnp.maximum(-wx, 0.0) + jnp.log1p(jnp.exp(-jnp.abs(wx)))) - 0.5
    w_ref[0] = jnp.exp(-jnp.exp(w_log))

    al = jnp.dot(mix(5), a1_ref[...], preferred_element_type=F32)
    a = jax.nn.sigmoid(a0_ref[...] + _dot(al, a2_ref[...]))

    kk = k * kk_ref[...]
    n2 = _segsum(kk * kk, seg_ref[...])
    inv = 1.0 / jnp.maximum(jnp.sqrt(n2), 1e-12)
    kk = kk * _segbcast(inv, bc_ref[...])
    kh = k * (1.0 + (a - 1.0) * ka_ref[...])
    k_ref[0] = kh
    a_ref[0] = -kk
    b_ref[0] = kk * a
    rk = _segsum(r * kh * rk_ref[...], seg_ref[...])
    bonus_ref[0] = _segbcast(rk, bc_ref[...]) * v


def _layer_c1(x, sh0, g, mu, w_in, w0, w1, w2, a0, a1, a2, k_k, k_a, r_k, *, tm):
    B, T, D = x.shape
    seg, bc = _seg_mats()
    row = lambda v: v.reshape(1, -1)
    tile = pl.BlockSpec((1, tm, D), lambda b, t: (b, t, 0))
    outs = pl.pallas_call(
        functools.partial(_layer_c1_kernel, tm=tm),
        grid=(B, T // tm),
        in_specs=[
            tile,
            pl.BlockSpec((1, 1, D), lambda b, t: (b, 0, 0)),
            _const_spec((1, D)),
            _const_spec((6, D)),
            _const_spec((D, 4 * D)),
            _const_spec((1, D)),
            _const_spec((D, D_LORA)),
            _const_spec((D_LORA, D)),
            _const_spec((1, D)),
            _const_spec((D, D_LORA)),
            _const_spec((D_LORA, D)),
            _const_spec((1, D)),
            _const_spec((1, D)),
            _const_spec((1, D)),
            _const_spec((D, LANES)),
            _const_spec((LANES, D)),
        ],
        out_specs=[tile] * 8 + [pl.BlockSpec((1, 1, D), lambda b, t: (b, 0, 0))],
        out_shape=[jax.ShapeDtypeStruct((B, T, D), F32)] * 8 + [jax.ShapeDtypeStruct((B, 1, D), F32)],
        scratch_shapes=[pltpu.VMEM((1, D), F32)],
        compiler_params=pltpu.CompilerParams(
            dimension_semantics=("arbitrary", "arbitrary"), vmem_limit_bytes=VMEM_LIMIT),
        name="layer_c1",
    )(x, sh0.reshape(B, 1, D), row(g), mu, w_in, row(w0), w1, w2, row(a0), a1, a2, row(k_k), row(k_a), row(r_k),
      seg, bc)
    return outs[:8], outs[8].reshape(B, D)


def _layer_c3_kernel(x_ref, o_ref, bonus_ref, gt_ref, gng_ref, gnb_ref, wout_ref, seg_ref, bc_ref, xo_ref):
    o = o_ref[0]
    mu = _segsum(o, seg_ref[...]) * (1.0 / RWKV_HEAD)
    d = o - _segbcast(mu, bc_ref[...])
    var = _segsum(d * d, seg_ref[...]) * (1.0 / RWKV_HEAD)
    o = d * _segbcast(lax.rsqrt(var + RWKV_GN_EPS), bc_ref[...]) * gng_ref[...] + gnb_ref[...]
    y = (o + bonus_ref[0]) * _silu(gt_ref[0])
    xo_ref[0] = x_ref[0] + _dot(y, wout_ref[...])


def _layer_c3(x, o, bonus, gt, gn_g, gn_b, w_out, *, tm):
    B, T, D = x.shape
    seg, bc = _seg_mats()
    row = lambda v: v.reshape(1, -1)
    tile = pl.BlockSpec((1, tm, D), lambda b, t: (b, t, 0))
    return pl.pallas_call(
        _layer_c3_kernel,
        grid=(B, T // tm),
        in_specs=[tile, tile, tile, tile, _const_spec((1, D)), _const_spec((1, D)), _const_spec((D, D)),
                  _const_spec((D, LANES)), _const_spec((LANES, D))],
        out_specs=tile,
        out_shape=jax.ShapeDtypeStruct((B, T, D), F32),
        compiler_params=pltpu.CompilerParams(
            dimension_semantics=("arbitrary", "arbitrary"), vmem_limit_bytes=VMEM_LIMIT),
        name="layer_c3",
    )(x, o, bonus, gt, row(gn_g), row(gn_b), w_out, seg, bc)


def _wkv_kernel(r_ref, w_ref, k_ref, a_ref, b_ref, v_ref, s0_ref, y_ref, so_ref,
                col_ref, vt_ref, yt_ref, *, tq, nb, packed):
    @pl.when(pl.program_id(0) == 0)
    def _():
        so_ref[...] = s0_ref[...]

    half = LANES // 2
    npar = 2 if packed else 1
    ngrp = tq // npar
    nrow = RWKV_HEAD // npar
    low = lax.broadcasted_iota(jnp.int32, (RWKV_HEAD, LANES), 1) < half
    low_r = lax.broadcasted_iota(jnp.int32, (nrow, LANES), 1) < half
    srcs = (r_ref, w_ref, k_ref, a_ref, b_ref)
    Q_R, Q_W, Q_K, Q_A, Q_B = range(5)

    def relayout(g):
        t0 = npar * g

        def gathered(ref):
            return jnp.concatenate([ref[b, t0 + par] for par in range(npar) for b in range(nb)], axis=0).T

        if packed:
            for q in range(5):
                mt = gathered(srcs[q])
                rolled = pltpu.roll(mt, half, 1)
                col_ref[t0, q] = jnp.where(low, mt, rolled)
                col_ref[t0 + 1, q] = jnp.where(low, rolled, mt)
            mt = gathered(v_ref)
            top, bot = mt[0:nrow], mt[nrow:RWKV_HEAD]
            vt_ref[t0] = jnp.where(low_r, top, pltpu.roll(bot, half, 1))
            vt_ref[t0 + 1] = jnp.where(low_r, pltpu.roll(top, half, 1), bot)
        else:
            for q in range(5):
                col_ref[t0, q] = gathered(srcs[q])
            vt_ref[t0] = gathered(v_ref)

    def writeback(g):
        if packed:
            y0, y1 = yt_ref[npar * g], yt_ref[npar * g + 1]
            yt = jnp.concatenate([jnp.where(low_r, y0, pltpu.roll(y1, half, 1)),
                                  jnp.where(low_r, pltpu.roll(y0, half, 1), y1)], axis=0).T
        else:
            yt = yt_ref[g].T
        for par in range(npar):
            for b in range(nb):
                r0 = (par * nb + b) * RWKV_HEADS
                y_ref[b, npar * g + par] = yt[r0:r0 + RWKV_HEADS, :]

    def token(t, sa):
        tn = jnp.minimum(t + 1, tq - 1)
        vt = vt_ref[t]
        y, nxt = [None, None], [None, None]
        for j in range(RWKV_HEAD):
            row = lambda q: col_ref[t, q, j:j + 1, :]
            s = so_ref[j] * row(Q_W) + sa * row(Q_B) + vt * row(Q_K)
            so_ref[j] = s
            yj, nj = s * row(Q_R), s * col_ref[tn, Q_A, j:j + 1, :]
            y[j % 2] = yj if y[j % 2] is None else y[j % 2] + yj
            nxt[j % 2] = nj if nxt[j % 2] is None else nxt[j % 2] + nj
        yt_ref[t] = y[0] + y[1]
        return nxt[0] + nxt[1]

    def relayout2(u, carry):
        relayout(2 * u)
        relayout(2 * u + 1)
        return carry

    lax.fori_loop(0, ngrp // 2, relayout2, 0)

    acc = [so_ref[0] * col_ref[0, Q_A, 0:1, :], so_ref[1] * col_ref[0, Q_A, 1:2, :]]
    for j in range(2, RWKV_HEAD):
        acc[j % 2] = acc[j % 2] + so_ref[j] * col_ref[0, Q_A, j:j + 1, :]
    lax.fori_loop(0, tq, token, acc[0] + acc[1])

    def writeback4(u, carry):
        for n in range(4):
            writeback(4 * u + n)
        return carry

    lax.fori_loop(0, ngrp // 4, writeback4, 0)


def _wkv_layer(r, w, k, v, a, b, s0):
    B, T, D = r.shape
    H, N = RWKV_HEADS, RWKV_HEAD
    packed = B * H == LANES // 2
    assert packed or B * H == LANES
    if packed:
        ni = N // 2
        st_in = s0.reshape(B, H, 2, ni, N).transpose(4, 3, 2, 0, 1).reshape(N, ni, LANES)
        st_out = lambda s: s.reshape(N, ni, 2, B, H).transpose(3, 4, 2, 1, 0).reshape(B, H, N, N)
    else:
        ni = N
        st_in = s0.transpose(3, 2, 0, 1).reshape(N, N, LANES)
        st_out = lambda s: s.reshape(N, N, B, H).transpose(2, 3, 1, 0)
    tq = min(T, 64)
    heads = lambda z: z.reshape(B, T, H, N)
    tok_spec = pl.BlockSpec((B, tq, H, N), lambda t: (0, t, 0, 0))
    st_spec = pl.BlockSpec((N, ni, LANES), lambda t: (0, 0, 0))
    npar = 2 if packed else 1
    y, s = pl.pallas_call(
        functools.partial(_wkv_kernel, tq=tq, nb=B, packed=packed),
        grid=(T // tq,),
        in_specs=[tok_spec] * 6 + [st_spec],
        out_specs=[tok_spec, st_spec],
        out_shape=[jax.ShapeDtypeStruct((B, T, H, N), F32), jax.ShapeDtypeStruct((N, ni, LANES), F32)],
        scratch_shapes=[pltpu.VMEM((tq, 5, N, LANES), F32), pltpu.VMEM((tq, ni, LANES), F32),
                        pltpu.VMEM((tq, ni, LANES), F32)],
        compiler_params=pltpu.CompilerParams(dimension_semantics=("arbitrary",), vmem_limit_bytes=VMEM_LIMIT),
        name="wkv",
    )(heads(r), heads(w), heads(k), heads(a), heads(b), heads(v), st_in)
    return y.reshape(B, T, D), st_out(s)


def kernel(x_prompt, x_sample, cache_conv_a, cache_conv_b, state_lru_b, state_shift_c, state_wkv_c, state_ret_d, meta_tokens, norm_g, final_norm_g, a_w_in, a_w_dw, a_b_dw, a_ln_g, a_ln_b, a_w_out, b_w_in, b_w_conv, b_b_conv, b_w_rg, b_b_rg, b_w_ig, b_b_ig, b_lam, b_w_out, c_mu, c_w_in, c_w0, c_w1, c_w2, c_a0, c_a1, c_a2, c_k_k, c_k_a, c_r_k, c_gn_g, c_gn_b, c_w_out, d_w_in, d_gn_g, d_w_out):
    bf = lambda z: z.astype(BF16)
    a_win, a_wout = bf(a_w_in), bf(a_w_out)
    b_win, b_wout = bf(b_w_in), bf(b_w_out)
    b_wg = bf(jnp.concatenate([b_w_rg, b_w_ig], axis=-1))
    c_win, c_wout = bf(c_w_in), bf(c_w_out)
    c_w1b, c_w2b, c_a1b, c_a2b = bf(c_w1), bf(c_w2), bf(c_a1), bf(c_a2)
    d_win, d_wout = bf(d_w_in), bf(d_w_out)

    def trunk(x, st, pos0, tm):
        x, conv_a = _layer_a(x, st[0], norm_g[0], a_win, a_w_dw, a_b_dw, a_ln_g, a_ln_b, a_wout, tm=tm)
        x, conv_b, lru = _layer_b(x, st[1], st[2], norm_g[1], b_win, b_w_conv, b_b_conv, b_wg, b_b_rg, b_b_ig,
                                  b_lam, b_wout, tm=tm)
        (r, w, k, v, a, b, gt, bonus), shift = _layer_c1(
            x, st[3], norm_g[2], c_mu, c_win, c_w0, c_w1b, c_w2b, c_a0, c_a1b, c_a2b, c_k_k, c_k_a,
            c_r_k.reshape(-1), tm=tm)
        o, wkv = _wkv_layer(r, w, k, v, a, b, st[4])
        x = _layer_c3(x, o, bonus, gt, c_gn_g, c_gn_b, c_wout, tm=tm)
        x, ret = _layer_d(x, st[5], pos0, norm_g[3], d_win, d_gn_g, d_wout, final_norm_g, tm=tm)
        return x, (conv_a, conv_b, lru, shift, wkv, ret)

    B = x_prompt.shape[0]
    zeros = (jnp.zeros((B, CONV_W - 1, D_MODEL), F32), jnp.zeros((B, LRU_CONV_W - 1, D_RNN), F32),
             jnp.zeros((B, D_RNN), F32), jnp.zeros((B, D_MODEL), F32),
             jnp.zeros((B, RWKV_HEADS, RWKV_HEAD, RWKV_HEAD), F32), jnp.zeros((B, RET_HEADS, RET_DK, RET_DV), F32))
    meta = jnp.broadcast_to(meta_tokens[None], (B, N_META, D_MODEL))
    _, st_meta = trunk(meta, zeros, 0, N_META)
    y_prompt, stp = trunk(x_prompt, st_meta, N_META, 256)
    st_s = (cache_conv_a, cache_conv_b, state_lru_b, state_shift_c, state_wkv_c, state_ret_d)
    y_sample, sts = trunk(x_sample, st_s, N_META + PAST_LEN, x_sample.shape[1])
    return (y_prompt, y_sample, stp[0], sts[0], stp[1], sts[1], stp[2], sts[2], stp[3], sts[3],
            stp[4], sts[4], stp[5], sts[5])
```

```python
import functools
import math

import jax
import jax.numpy as jnp
from jax import lax
from jax.experimental import pallas as pl
from jax.experimental.pallas import tpu as pltpu

F32 = jnp.float32
BF16 = jnp.bfloat16

D_MODEL = 1024
N_META = 16
PAST_LEN = 1024
NORM_EPS = 1e-6
LN_EPS = 1e-5
CONV_W = 31
HIST = 32
LANES = 128
SUBLANES = 8

VMEM_LIMIT = 56 * 1024 * 1024


def _rms(x, g):
    return x * lax.rsqrt(jnp.mean(x * x, axis=-1, keepdims=True) + NORM_EPS) * g


def _silu(x):
    return x * jax.nn.sigmoid(x)


def _dot(a, b):
    return jnp.dot(a.astype(BF16), b, preferred_element_type=F32)


def _dwconv_block(hist_ref, w_ref, b_ref, r0, rc, ls, width, nhist):
    off = nhist - (width - 1)
    nwin = rc + nhist
    win = hist_ref[r0:r0 + nwin, ls]
    acc = jnp.zeros((rc, ls.stop - ls.start), F32) + b_ref[:, ls]
    for res in range(SUBLANES):
        taps = [s for s in range(res, nhist + 1, SUBLANES) if 0 <= s - off < width]
        if not taps:
            continue
        wr = win if res == 0 else pltpu.roll(win, nwin - res, 0)
        for s in taps:
            acc = acc + w_ref[s - off:s - off + 1, ls] * wr[s - res:s - res + rc]
    return acc


def _const_spec(shape):
    n = len(shape)
    return pl.BlockSpec(shape, lambda b, t: (0,) * n, pipeline_mode=pl.Buffered(1))


def _layer_a_kernel(x_ref, buf0_ref, g_ref, win_ref, wdw_ref, bdw_ref, lng_ref, lnb_ref, wout_ref,
                    xo_ref, bufo_ref, proj_ref, hist_ref, c_ref, y_ref, *, tm):
    ti = pl.program_id(1)

    @pl.when(ti == 0)
    def _():
        hist_ref[0:HIST, :] = buf0_ref[0]

    x = x_ref[0]
    h = _rms(x, g_ref[...])
    proj_ref[...] = _dot(h, win_ref[...])
    hist_ref[HIST:HIST + tm, :] = proj_ref[:, 0:D_MODEL] * jax.nn.sigmoid(proj_ref[:, D_MODEL:2 * D_MODEL])

    rc = min(tm, 64)
    for r in range(tm // rc):
        for lb in range(D_MODEL // LANES):
            ls = slice(lb * LANES, (lb + 1) * LANES)
            c_ref[r * rc:(r + 1) * rc, ls] = _dwconv_block(hist_ref, wdw_ref, bdw_ref, r * rc, rc, ls, CONV_W, HIST)

    rc = min(tm, 16)
    for r in range(tm // rc):
        acc = c_ref[r * rc:(r + 1) * rc, :]
        mu = jnp.mean(acc, axis=-1, keepdims=True)
        d = acc - mu
        var = jnp.mean(d * d, axis=-1, keepdims=True)
        c = d * lax.rsqrt(var + LN_EPS) * lng_ref[...] + lnb_ref[...]
        gate = proj_ref[r * rc:(r + 1) * rc, 2 * D_MODEL:3 * D_MODEL]
        y_ref[r * rc:(r + 1) * rc, :] = (_silu(c) * _silu(gate)).astype(BF16)

    xo_ref[0] = x + jnp.dot(y_ref[...], wout_ref[...], preferred_element_type=F32)

    if tm >= HIST:
        hist_ref[0:HIST, :] = hist_ref[tm:tm + HIST, :]
    else:
        hist_ref[0:HIST - tm, :] = hist_ref[tm:HIST, :]
        hist_ref[HIST - tm:HIST, :] = hist_ref[HIST:HIST + tm, :]
    bufo_ref[0] = hist_ref[0:HIST, :]


def _layer_a(x, buf, g, w_in, w_dw, b_dw, ln_g, ln_b, w_out, *, tm):
    B, T, D = x.shape
    buf32 = jnp.concatenate([jnp.zeros((B, HIST - (CONV_W - 1), D), F32), buf], axis=1)
    row = lambda v: v.reshape(1, -1)
    xo, bufo = pl.pallas_call(
        functools.partial(_layer_a_kernel, tm=tm),
        grid=(B, T // tm),
        in_specs=[
            pl.BlockSpec((1, tm, D), lambda b, t: (b, t, 0)),
            pl.BlockSpec((1, HIST, D), lambda b, t: (b, 0, 0)),
            _const_spec((1, D)),
            _const_spec((D, 3 * D)),
            _const_spec((CONV_W, D)),
            _const_spec((1, D)),
            _const_spec((1, D)),
            _const_spec((1, D)),
            _const_spec((D, D)),
        ],
        out_specs=[
            pl.BlockSpec((1, tm, D), lambda b, t: (b, t, 0)),
            pl.BlockSpec((1, HIST, D), lambda b, t: (b, 0, 0)),
        ],
        out_shape=[jax.ShapeDtypeStruct((B, T, D), F32), jax.ShapeDtypeStruct((B, HIST, D), F32)],
        scratch_shapes=[
            pltpu.VMEM((tm, 3 * D), F32),
            pltpu.VMEM((HIST + tm, D), F32),
            pltpu.VMEM((tm, D), F32),
            pltpu.VMEM((tm, D), BF16),
        ],
        compiler_params=pltpu.CompilerParams(
            dimension_semantics=("arbitrary", "arbitrary"), vmem_limit_bytes=VMEM_LIMIT),
        name="layer_a",
    )(x, buf32, row(g), w_in, w_dw, row(b_dw), row(ln_g), row(ln_b), w_out)
    return xo, bufo[:, HIST - (CONV_W - 1):]


D_RNN = 1280
LRU_BLOCKS = 10
LRU_BLOCK = 128
LRU_CONV_W = 4
LRU_C = 8.0
BHIST = 8


def _scan_affine_rows(a, b):
    n = a.shape[0]
    rows = lax.broadcasted_iota(jnp.int32, a.shape, 0)
    k = 1
    while k < n:
        a_s = jnp.where(rows >= k, pltpu.roll(a, k, 0), 1.0)
        b_s = jnp.where(rows >= k, pltpu.roll(b, k, 0), 0.0)
        b = a * b_s + b
        a = a * a_s
        k *= 2
    return a, b


def _one_minus_exp(z, ez):
    s = 1.0 / 120.0
    for c in (1.0 / 24.0, 1.0 / 6.0, 0.5, 1.0):
        s = c + z * s
    return jnp.where(z > -1.0 / 32.0, -z * s, 1.0 - ez)


def _sqrt_nonneg(x):
    return jnp.where(x > 0.0, x * lax.rsqrt(x), 0.0)


def _layer_b_kernel(x_ref, buf0_ref, h0_ref, g_ref, win_ref, wc_ref, bc_ref, wg_ref, brg_ref, big_ref, lam_ref,
                    wout_ref, xo_ref, bufo_ref, ho_ref, proj_ref, hist_ref, carry_ref, y_ref, *, tm):
    ti = pl.program_id(1)

    @pl.when(ti == 0)
    def _():
        hist_ref[0:BHIST, :] = buf0_ref[0]
        carry_ref[...] = h0_ref[0]

    x = x_ref[0]
    h = _rms(x, g_ref[...])
    proj_ref[...] = _dot(h, win_ref[...])
    hist_ref[BHIST:BHIST + tm, :] = proj_ref[:, 0:D_RNN]

    lam = lam_ref[...]
    neg_c_sp = -LRU_C * (jnp.maximum(-lam, 0.0) + jnp.log1p(jnp.exp(-jnp.abs(lam))))

    rc = min(tm, 64)
    for r in range(tm // rc):
        r0 = r * rc
        xc = jnp.concatenate(
            [_dwconv_block(hist_ref, wc_ref, bc_ref, r0, rc, slice(lb * LANES, (lb + 1) * LANES), LRU_CONV_W, BHIST)
             for lb in range(D_RNN // LANES)], axis=1)
        xcb = xc.astype(BF16)
        rg, ig = [], []
        for n in range(LRU_BLOCKS):
            gg = jnp.dot(xcb[:, n * LRU_BLOCK:(n + 1) * LRU_BLOCK], wg_ref[n], preferred_element_type=F32)
            rg.append(gg[:, 0:LRU_BLOCK])
            ig.append(gg[:, LRU_BLOCK:2 * LRU_BLOCK])
        rgate = jax.nn.sigmoid(jnp.concatenate(rg, axis=1) + brg_ref[...])
        igate = jax.nn.sigmoid(jnp.concatenate(ig, axis=1) + big_ref[...])
        log_a = neg_c_sp * rgate
        a = jnp.exp(log_a)
        bx = _sqrt_nonneg(_one_minus_exp(2.0 * log_a, a * a)) * (igate * xc)
        a_cum, b_cum = _scan_affine_rows(a, bx)
        hs = a_cum * carry_ref[...] + b_cum
        carry_ref[...] = hs[rc - 1:rc, :]
        gate = proj_ref[r0:r0 + rc, D_RNN:2 * D_RNN]
        y_ref[r0:r0 + rc, :] = (hs * _silu(gate)).astype(BF16)

    xo_ref[0] = x + jnp.dot(y_ref[...], wout_ref[...], preferred_element_type=F32)
    hist_ref[0:BHIST, :] = hist_ref[tm:tm + BHIST, :]
    bufo_ref[0] = hist_ref[0:BHIST, :]
    ho_ref[0] = carry_ref[...]


def _layer_b(x, buf, h0, g, w_in, w_conv, b_conv, w_g, b_rg, b_ig, lam, w_out, *, tm):
    B, T, D = x.shape
    buf8 = jnp.concatenate([jnp.zeros((B, BHIST - (LRU_CONV_W - 1), D_RNN), F32), buf], axis=1)
    row = lambda v: v.reshape(1, -1)
    xo, bufo, ho = pl.pallas_call(
        functools.partial(_layer_b_kernel, tm=tm),
        grid=(B, T // tm),
        in_specs=[
            pl.BlockSpec((1, tm, D), lambda b, t: (b, t, 0)),
            pl.BlockSpec((1, BHIST, D_RNN), lambda b, t: (b, 0, 0)),
            pl.BlockSpec((1, 1, D_RNN), lambda b, t: (b, 0, 0)),
            _const_spec((1, D)),
            _const_spec((D, 2 * D_RNN)),
            _const_spec((LRU_CONV_W, D_RNN)),
            _const_spec((1, D_RNN)),
            _const_spec((LRU_BLOCKS, LRU_BLOCK, 2 * LRU_BLOCK)),
            _const_spec((1, D_RNN)),
            _const_spec((1, D_RNN)),
            _const_spec((1, D_RNN)),
            _const_spec((D_RNN, D)),
        ],
        out_specs=[
            pl.BlockSpec((1, tm, D), lambda b, t: (b, t, 0)),
            pl.BlockSpec((1, BHIST, D_RNN), lambda b, t: (b, 0, 0)),
            pl.BlockSpec((1, 1, D_RNN), lambda b, t: (b, 0, 0)),
        ],
        out_shape=[jax.ShapeDtypeStruct((B, T, D), F32), jax.ShapeDtypeStruct((B, BHIST, D_RNN), F32),
                   jax.ShapeDtypeStruct((B, 1, D_RNN), F32)],
        scratch_shapes=[
            pltpu.VMEM((tm, 2 * D_RNN), F32),
            pltpu.VMEM((BHIST + tm, D_RNN), F32),
            pltpu.VMEM((1, D_RNN), F32),
            pltpu.VMEM((tm, D_RNN), BF16),
        ],
        compiler_params=pltpu.CompilerParams(
            dimension_semantics=("arbitrary", "arbitrary"), vmem_limit_bytes=VMEM_LIMIT),
        name="layer_b",
    )(x, buf8, h0.reshape(B, 1, D_RNN), row(g), w_in, w_conv, row(b_conv), w_g, row(b_rg), row(b_ig), row(lam), w_out)
    return xo, bufo[:, BHIST - (LRU_CONV_W - 1):], ho.reshape(B, D_RNN)


RET_HEADS = 4
RET_DK = 256
RET_DV = 512
D_RET_V = RET_HEADS * RET_DV
ROPE_BASE = 10000.0


def _rot(z, cos, sin):
    half = RET_DK // 2
    z1, z2 = z[:, 0:half], z[:, half:RET_DK]
    return jnp.concatenate([z1 * cos - z2 * sin, z1 * sin + z2 * cos], axis=1)


def _layer_d_kernel(x_ref, s0_ref, cos_ref, sin_ref, g_ref, win_ref, gng_ref, wout_ref, fg_ref,
                    xo_ref, so_ref, proj_ref, y_ref, *, tm, lgs):
    ti = pl.program_id(1)

    @pl.when(ti == 0)
    def _():
        so_ref[...] = s0_ref[...]

    x = x_ref[0]
    h = _rms(x, g_ref[...])
    proj_ref[...] = _dot(h, win_ref[...])
    cos = cos_ref[...]
    sin = sin_ref[...]
    row = lax.broadcasted_iota(jnp.int32, (tm, tm), 0)
    col = lax.broadcasted_iota(jnp.int32, (tm, tm), 1)
    causal = row >= col
    diff = jnp.where(causal, row - col, 0).astype(F32)
    idx = lax.broadcasted_iota(jnp.int32, (tm, 1), 0).astype(F32)

    for hh in range(RET_HEADS):
        lg = lgs[hh]
        q = _rot(proj_ref[:, hh * RET_DK:(hh + 1) * RET_DK], cos, sin) * (RET_DK ** -0.5)
        k = _rot(proj_ref[:, D_MODEL + hh * RET_DK:D_MODEL + (hh + 1) * RET_DK], cos, sin)
        v = proj_ref[:, 2 * D_MODEL + hh * RET_DV:2 * D_MODEL + (hh + 1) * RET_DV].astype(BF16)
        qb = q.astype(BF16)
        dmask = jnp.where(causal, jnp.exp(diff * lg), 0.0)
        scores = lax.dot_general(qb, k.astype(BF16), (((1,), (1,)), ((), ())), preferred_element_type=F32) * dmask
        s_old = so_ref[0, hh]
        o = jnp.dot(scores.astype(BF16), v, preferred_element_type=F32)
        o = o + jnp.dot(qb, s_old.astype(BF16), preferred_element_type=F32) * jnp.exp((idx + 1.0) * lg)
        k_dec = (k * jnp.exp((tm - 1.0 - idx) * lg)).astype(BF16)
        so_ref[0, hh] = s_old * jnp.exp(jnp.full((1, 1), tm * lg, F32)) + lax.dot_general(
            k_dec, v, (((0,), (0,)), ((), ())), preferred_element_type=F32)
        o = o * lax.rsqrt(jnp.mean(o * o, axis=-1, keepdims=True) + NORM_EPS)
        o = o * gng_ref[:, hh * RET_DV:(hh + 1) * RET_DV]
        gate = proj_ref[:, 2 * D_MODEL + D_RET_V + hh * RET_DV:2 * D_MODEL + D_RET_V + (hh + 1) * RET_DV]
        y_ref[:, hh * RET_DV:(hh + 1) * RET_DV] = (o * _silu(gate)).astype(BF16)

    xn = x + jnp.dot(y_ref[...], wout_ref[...], preferred_element_type=F32)
    xo_ref[0] = _rms(xn, fg_ref[...])


def _layer_d(x, s0, pos0, g, w_in, gn_g, w_out, final_g, *, tm):
    B, T, D = x.shape
    lgs =tuple(math.log1p(-(2.0 ** (-5.0 - hh))) for hh in range(RET_HEADS))
    half = RET_DK // 2
    inv = ROPE_BASE ** (-jnp.arange(half, dtype=F32) / half)
    ang = (pos0 + jnp.arange(T)).astype(F32)[:, None] * inv[None]
    cos, sin = jnp.cos(ang), jnp.sin(ang)
    row = lambda v: v.reshape(1, -1)
    nproj = 2 * D + 2 * D_RET_V
    xo, so = pl.pallas_call(
        functools.partial(_layer_d_kernel, tm=tm, lgs=lgs),
        grid=(B, T // tm),
        in_specs=[
            pl.BlockSpec((1, tm, D), lambda b, t: (b, t, 0)),
            pl.BlockSpec((1, RET_HEADS, RET_DK, RET_DV), lambda b, t: (b, 0, 0, 0)),
            pl.BlockSpec((tm, half), lambda b, t: (t, 0)),
            pl.BlockSpec((tm, half), lambda b, t: (t, 0)),
            _const_spec((1, D)),
            _const_spec((D, nproj)),
            _const_spec((1, D_RET_V)),
            _const_spec((D_RET_V, D)),
            _const_spec((1, D)),
        ],
        out_specs=[
            pl.BlockSpec((1, tm, D), lambda b, t: (b, t, 0)),
            pl.BlockSpec((1, RET_HEADS, RET_DK, RET_DV), lambda b, t: (b, 0, 0, 0)),
        ],
        out_shape=[jax.ShapeDtypeStruct((B, T, D), F32),
                   jax.ShapeDtypeStruct((B, RET_HEADS, RET_DK, RET_DV), F32)],
        scratch_shapes=[
            pltpu.VMEM((tm, nproj), F32),
            pltpu.VMEM((tm, D_RET_V), BF16),
        ],
        compiler_params=pltpu.CompilerParams(
            dimension_semantics=("arbitrary", "arbitrary"), vmem_limit_bytes=VMEM_LIMIT),
        name="layer_d",
    )(x, s0, cos, sin, row(g), w_in, row(gn_g), w_out, row(final_g))
    return xo, so


RWKV_HEAD = 64
RWKV_HEADS = 16
D_LORA = 64
RWKV_GN_EPS = 64e-5


def _seg_mats():
    c = jnp.arange(D_MODEL)[:, None] // RWKV_HEAD
    hcol = jnp.arange(LANES)[None, :]
    seg = (c == hcol).astype(BF16)
    return seg, seg.T


def _segsum(z, seg):
    return jnp.dot(z.astype(BF16), seg, preferred_element_type=F32)


def _segbcast(s, bc):
    hi = s.astype(BF16)
    lo = (s - hi.astype(F32)).astype(BF16)
    return jnp.dot(hi, bc, preferred_element_type=F32) + jnp.dot(lo, bc, preferred_element_type=F32)


def _layer_c1_kernel(x_ref, sh0_ref, g_ref, mu_ref, win_ref, w0_ref, w1_ref, w2_ref, a0_ref, a1_ref, a2_ref,
                     kk_ref, ka_ref, rk_ref, seg_ref, bc_ref,
                     r_ref, w_ref, k_ref, v_ref, a_ref, b_ref, gt_ref, bonus_ref, sho_ref, carry_ref, *, tm):
    ti = pl.program_id(1)

    @pl.when(ti == 0)
    def _():
        carry_ref[...] = sh0_ref[0]

    h = _rms(x_ref[0], g_ref[...])
    rows = lax.broadcasted_iota(jnp.int32, h.shape, 0)
    h_prev = jnp.where(rows == 0, carry_ref[...], pltpu.roll(h, 1, 0))
    carry_ref[...] = h[tm - 1:tm, :]
    sho_ref[0] = h[tm - 1:tm, :]
    xx = h_prev - h

    def mix(s):
        return (h + xx * mu_ref[s:s + 1, :]).astype(BF16)

    r = jnp.dot(mix(0), win_ref[:, 0:D_MODEL], preferred_element_type=F32)
    k = jnp.dot(mix(1), win_ref[:, D_MODEL:2 * D_MODEL], preferred_element_type=F32)
    v = jnp.dot(mix(2), win_ref[:, 2 * D_MODEL:3 * D_MODEL], preferred_element_type=F32)
    gt_ref[0] = jnp.dot(mix(3), win_ref[:, 3 * D_MODEL:4 * D_MODEL], preferred_element_type=F32)
    r_ref[0] = r
    v_ref[0] = v

    wl = jnp.tanh(jnp.dot(mix(4), w1_ref[...], preferred_element_type=F32))
    wx = w0_ref[...] + _dot(wl, w2_ref[...])
    w_log = -(jnp.maximum(-wx, 0.0) + jnp.log1p(jnp.exp(-jnp.abs(wx)))) - 0.5
    w_ref[0] = jnp.exp(-jnp.exp(w_log))

    al = jnp.dot(mix(5), a1_ref[...], preferred_element_type=F32)
    a = jax.nn.sigmoid(a0_ref[...] + _dot(al, a2_ref[...]))

    kk = k * kk_ref[...]
    n2 = _segsum(kk * kk, seg_ref[...])
    inv = 1.0 / jnp.maximum(jnp.sqrt(n2), 1e-12)
    kk = kk * _segbcast(inv, bc_ref[...])
    kh = k * (1.0 + (a - 1.0) * ka_ref[...])
    k_ref[0] = kh
    a_ref[0] = -kk
    b_ref[0] = kk * a
    rk = _segsum(r * kh * rk_ref[...], seg_ref[...])
    bonus_ref[0] = _segbcast(rk, bc_ref[...]) * v


def _layer_c1(x, sh0, g, mu, w_in, w0, w1, w2, a0, a1, a2, k_k, k_a, r_k, *, tm):
    B, T, D = x.shape
    seg, bc = _seg_mats()
    row = lambda v: v.reshape(1, -1)
    tile = pl.BlockSpec((1, tm, D), lambda b, t: (b, t, 0))
    outs = pl.pallas_call(
        functools.partial(_layer_c1_kernel, tm=tm),
        grid=(B, T // tm),
        in_specs=[
            tile,
            pl.BlockSpec((1, 1, D), lambda b, t: (b, 0, 0)),
            _const_spec((1, D)),
            _const_spec((6, D)),
            _const_spec((D, 4 * D)),
            _const_spec((1, D)),
            _const_spec((D, D_LORA)),
            _const_spec((D_LORA, D)),
            _const_spec((1, D)),
            _const_spec((D, D_LORA)),
            _const_spec((D_LORA, D)),
            _const_spec((1, D)),
            _const_spec((1, D)),
            _const_spec((1, D)),
            _const_spec((D, LANES)),
            _const_spec((LANES, D)),
        ],
        out_specs=[tile] * 8 + [pl.BlockSpec((1, 1, D), lambda b, t: (b, 0, 0))],
        out_shape=[jax.ShapeDtypeStruct((B, T, D), F32)] * 8 + [jax.ShapeDtypeStruct((B, 1, D), F32)],
        scratch_shapes=[pltpu.VMEM((1, D), F32)],
        compiler_params=pltpu.CompilerParams(
            dimension_semantics=("arbitrary", "arbitrary"), vmem_limit_bytes=VMEM_LIMIT),
        name="layer_c1",
    )(x, sh0.reshape(B, 1, D), row(g), mu, w_in, row(w0), w1, w2, row(a0), a1, a2, row(k_k), row(k_a), row(r_k),
      seg, bc)
    return outs[:8], outs[8].reshape(B, D)


def _layer_c3_kernel(x_ref, o_ref, bonus_ref, gt_ref, gng_ref, gnb_ref, wout_ref, seg_ref, bc_ref, xo_ref):
    o = o_ref[0]
    mu = _segsum(o, seg_ref[...]) * (1.0 / RWKV_HEAD)
    d = o - _segbcast(mu, bc_ref[...])
    var = _segsum(d * d, seg_ref[...]) * (1.0 / RWKV_HEAD)
    o = d * _segbcast(lax.rsqrt(var + RWKV_GN_EPS), bc_ref[...]) * gng_ref[...] + gnb_ref[...]
    y = (o + bonus_ref[0]) * _silu(gt_ref[0])
    xo_ref[0] = x_ref[0] + _dot(y, wout_ref[...])


def _layer_c3(x, o, bonus, gt, gn_g, gn_b, w_out, *, tm):
    B, T, D = x.shape
    seg, bc = _seg_mats()
    row = lambda v: v.reshape(1, -1)
    tile = pl.BlockSpec((1, tm, D), lambda b, t: (b, t, 0))
    return pl.pallas_call(
        _layer_c3_kernel,
        grid=(B, T // tm),
        in_specs=[tile, tile, tile, tile, _const_spec((1, D)), _const_spec((1, D)), _const_spec((D, D)),
                  _const_spec((D, LANES)), _const_spec((LANES, D))],
        out_specs=tile,
        out_shape=jax.ShapeDtypeStruct((B, T, D), F32),
        compiler_params=pltpu.CompilerParams(
            dimension_semantics=("arbitrary", "arbitrary"), vmem_limit_bytes=VMEM_LIMIT),
        name="layer_c3",
    )(x, o, bonus, gt, row(gn_g), row(gn_b), w_out, seg, bc)


def _wkv_kernel(r_ref, w_ref, k_ref, a_ref, b_ref, v_ref, s0_ref, y_ref, so_ref,
                col_ref, vt_ref, yt_ref, *, tq, nb, packed):
    @pl.when(pl.program_id(0) == 0)
    def _():
        so_ref[...] = s0_ref[...]

    half = LANES // 2
    npar = 2 if packed else 1
    ngrp = tq // npar
    nrow = RWKV_HEAD // npar
    low = lax.broadcasted_iota(jnp.int32, (RWKV_HEAD, LANES), 1) < half
    low_r = lax.broadcasted_iota(jnp.int32, (nrow, LANES), 1) < half
    srcs = (r_ref, w_ref, k_ref, a_ref, b_ref)
    Q_R, Q_W, Q_K, Q_A, Q_B = range(5)

    def relayout8(u, carry):
        t8 = pl.multiple_of(u * SUBLANES, SUBLANES)
        for q in range(6):
            ref = v_ref if q == 5 else srcs[q]
            slabs = [ref[b, pl.ds(t8, SUBLANES), :].reshape(SUBLANES, RWKV_HEADS, RWKV_HEAD) for b in range(nb)]
            for g in range(SUBLANES // npar):
                mt = jnp.concatenate([slabs[b][npar * g + par] for par in range(npar) for b in range(nb)], axis=0).T
                t0 = t8 + npar * g
                if not packed:
                    if q == 5:
                        vt_ref[t0] = mt
                    else:
                        col_ref[t0, q] = mt
                elif q == 5:
                    top, bot = mt[0:nrow], mt[nrow:RWKV_HEAD]
                    vt_ref[t0] = jnp.where(low_r, top, pltpu.roll(bot, half, 1))
                    vt_ref[t0 + 1] = jnp.where(low_r, pltpu.roll(top, half, 1), bot)
                else:
                    rolled = pltpu.roll(mt, half, 1)
                    col_ref[t0, q] = jnp.where(low, mt, rolled)
                    col_ref[t0 + 1, q] = jnp.where(low, rolled, mt)
        return carry

    def writeback8(u, carry):
        t8 = pl.multiple_of(u * SUBLANES, SUBLANES)
        slabs = [[None] * SUBLANES for _ in range(nb)]
        for g in range(SUBLANES // npar):
            t0 = t8 + npar * g
            if packed:
                y0, y1 = yt_ref[t0], yt_ref[t0 + 1]
                yt = jnp.concatenate([jnp.where(low_r, y0, pltpu.roll(y1, half, 1)),
                                      jnp.where(low_r, pltpu.roll(y0, half, 1), y1)], axis=0).T
            else:
                yt = yt_ref[t0].T
            for par in range(npar):
                for b in range(nb):
                    r0 = (par * nb + b) * RWKV_HEADS
                    slabs[b][npar * g + par] = yt[r0:r0 + RWKV_HEADS, :]
        for b in range(nb):
            y_ref[b, pl.ds(t8, SUBLANES), :] = jnp.stack(slabs[b], axis=0).reshape(SUBLANES, D_MODEL)
        return carry

    def token(t, sa):
        tn = jnp.minimum(t + 1, tq - 1)
        vt = vt_ref[t]
        y, nxt = [None, None], [None, None]
        for j in range(RWKV_HEAD):
            row = lambda q: col_ref[t, q, j:j + 1, :]
            s = so_ref[j] * row(Q_W) + sa * row(Q_B) + vt * row(Q_K)
            so_ref[j] = s
            yj, nj = s * row(Q_R), s * col_ref[tn, Q_A, j:j + 1, :]
            y[j % 2] = yj if y[j % 2] is None else y[j % 2] + yj
            nxt[j % 2] = nj if nxt[j % 2] is None else nxt[j % 2] + nj
        yt_ref[t] = y[0] + y[1]
        return nxt[0] + nxt[1]

    lax.fori_loop(0, tq // SUBLANES, relayout8, 0)

    acc = [so_ref[0] * col_ref[0, Q_A, 0:1, :], so_ref[1] * col_ref[0, Q_A, 1:2, :]]
    for j in range(2, RWKV_HEAD):
        acc[j % 2] = acc[j % 2] + so_ref[j] * col_ref[0, Q_A, j:j + 1, :]
    lax.fori_loop(0, tq, token, acc[0] + acc[1])

    lax.fori_loop(0, tq // SUBLANES, writeback8, 0)


def _wkv_layer(r, w, k, v, a, b, s0):
    B, T, D = r.shape
    H, N = RWKV_HEADS, RWKV_HEAD
    packed = B * H == LANES // 2
    assert packed or B * H == LANES
    if packed:
        ni = N // 2
        st_in = s0.reshape(B, H, 2, ni, N).transpose(4, 3, 2, 0, 1).reshape(N, ni, LANES)
        st_out = lambda s: s.reshape(N, ni, 2, B, H).transpose(3, 4, 2, 1, 0).reshape(B, H, N, N)
    else:
        ni = N
        st_in = s0.transpose(3, 2, 0, 1).reshape(N, N, LANES)
        st_out = lambda s: s.reshape(N, N, B, H).transpose(2, 3, 1, 0)
    tq = min(T, 64)
    tok_spec = pl.BlockSpec((B, tq, D), lambda t: (0, t, 0))
    st_spec = pl.BlockSpec((N, ni, LANES), lambda t: (0, 0, 0))
    y, s = pl.pallas_call(
        functools.partial(_wkv_kernel, tq=tq, nb=B, packed=packed),
        grid=(T // tq,),
        in_specs=[tok_spec] * 6 + [st_spec],
        out_specs=[tok_spec, st_spec],
        out_shape=[jax.ShapeDtypeStruct((B, T, D), F32), jax.ShapeDtypeStruct((N, ni, LANES), F32)],
        scratch_shapes=[pltpu.VMEM((tq, 5, N, LANES), F32), pltpu.VMEM((tq, ni, LANES), F32),
                        pltpu.VMEM((tq, ni, LANES), F32)],
        compiler_params=pltpu.CompilerParams(dimension_semantics=("arbitrary",), vmem_limit_bytes=VMEM_LIMIT),
        name="wkv",
    )(r, w, k, a, b, v, st_in)
    return y, st_out(s)


def kernel(x_prompt, x_sample, cache_conv_a, cache_conv_b, state_lru_b, state_shift_c, state_wkv_c, state_ret_d, meta_tokens, norm_g, final_norm_g, a_w_in, a_w_dw, a_b_dw, a_ln_g, a_ln_b, a_w_out, b_w_in, b_w_conv, b_b_conv, b_w_rg, b_b_rg, b_w_ig, b_b_ig, b_lam, b_w_out, c_mu, c_w_in, c_w0, c_w1, c_w2, c_a0, c_a1, c_a2, c_k_k, c_k_a, c_r_k, c_gn_g, c_gn_b, c_w_out, d_w_in, d_gn_g, d_w_out):
    bf = lambda z: z.astype(BF16)
    a_win, a_wout = bf(a_w_in), bf(a_w_out)
    b_win, b_wout = bf(b_w_in), bf(b_w_out)
    b_wg = bf(jnp.concatenate([b_w_rg, b_w_ig], axis=-1))
    c_win, c_wout = bf(c_w_in), bf(c_w_out)
    c_w1b, c_w2b, c_a1b, c_a2b = bf(c_w1), bf(c_w2), bf(c_a1), bf(c_a2)
    d_win, d_wout = bf(d_w_in), bf(d_w_out)

    def trunk(x, st, pos0, tm):
        x, conv_a = _layer_a(x, st[0], norm_g[0], a_win, a_w_dw, a_b_dw, a_ln_g, a_ln_b, a_wout, tm=tm)
        x, conv_b, lru = _layer_b(x, st[1], st[2], norm_g[1], b_win, b_w_conv, b_b_conv, b_wg, b_b_rg, b_b_ig,
                                  b_lam, b_wout, tm=tm)
        (r, w, k, v, a, b, gt, bonus), shift = _layer_c1(
            x, st[3], norm_g[2], c_mu, c_win, c_w0, c_w1b, c_w2b, c_a0, c_a1b, c_a2b, c_k_k, c_k_a,
            c_r_k.reshape(-1), tm=tm)
        o, wkv = _wkv_layer(r, w, k, v, a, b, st[4])
        x = _layer_c3(x, o, bonus, gt, c_gn_g, c_gn_b, c_wout, tm=tm)
        x, ret = _layer_d(x, st[5], pos0, norm_g[3], d_win, d_gn_g, d_wout, final_norm_g, tm=tm)
        return x, (conv_a, conv_b, lru, shift, wkv, ret)

    B = x_prompt.shape[0]
    zeros = (jnp.zeros((B, CONV_W - 1, D_MODEL), F32), jnp.zeros((B, LRU_CONV_W - 1, D_RNN), F32),
             jnp.zeros((B, D_RNN), F32), jnp.zeros((B, D_MODEL), F32),
             jnp.zeros((B, RWKV_HEADS, RWKV_HEAD, RWKV_HEAD), F32), jnp.zeros((B, RET_HEADS, RET_DK, RET_DV), F32))
    meta = jnp.broadcast_to(meta_tokens[None], (B, N_META, D_MODEL))
    _, st_meta = trunk(meta, zeros, 0, N_META)
    y_prompt, stp = trunk(x_prompt, st_meta, N_META, 256)
    st_s = (cache_conv_a, cache_conv_b, state_lru_b, state_shift_c, state_wkv_c, state_ret_d)
    y_sample, sts = trunk(x_sample, st_s, N_META + PAST_LEN, x_sample.shape[1])
    return (y_prompt, y_sample, stp[0], sts[0], stp[1], sts[1], stp[2], sts[2], stp[3], sts[3],
            stp[4], sts[4], stp[5], sts[5])
```

```python
import functools
import math

import jax
import jax.numpy as jnp
from jax import lax
from jax.experimental import pallas as pl
from jax.experimental.pallas import tpu as pltpu

F32 = jnp.float32
BF16 = jnp.bfloat16

D_MODEL = 1024
N_META = 16
PAST_LEN = 1024
NORM_EPS = 1e-6
LN_EPS = 1e-5
CONV_W = 31
HIST = 32
LANES = 128
SUBLANES = 8

VMEM_LIMIT = 56 * 1024 * 1024


def _rms(x, g):
    return x * lax.rsqrt(jnp.mean(x * x, axis=-1, keepdims=True) + NORM_EPS) * g


def _silu(x):
    return x * jax.nn.sigmoid(x)


def _dot(a, b):
    return jnp.dot(a.astype(BF16), b, preferred_element_type=F32)


def _dwconv_block(hist_ref, w_ref, b_ref, r0, rc, ls, width, nhist):
    off = nhist - (width - 1)
    nwin = rc + nhist
    win = hist_ref[r0:r0 + nwin, ls]
    acc = jnp.zeros((rc, ls.stop - ls.start), F32) + b_ref[:, ls]
    for res in range(SUBLANES):
        taps = [s for s in range(res, nhist + 1, SUBLANES) if 0 <= s - off < width]
        if not taps:
            continue
        wr = win if res == 0 else pltpu.roll(win, nwin - res, 0)
        for s in taps:
            acc = acc + w_ref[s - off:s - off + 1, ls] * wr[s - res:s - res + rc]
    return acc


def _const_spec(shape):
    n = len(shape)
    return pl.BlockSpec(shape, lambda b, t: (0,) * n, pipeline_mode=pl.Buffered(1))


def _layer_a_kernel(x_ref, buf0_ref, g_ref, win_ref, wdw_ref, bdw_ref, lng_ref, lnb_ref, wout_ref,
                    xo_ref, bufo_ref, proj_ref, hist_ref, c_ref, y_ref, *, tm):
    ti = pl.program_id(1)

    @pl.when(ti == 0)
    def _():
        hist_ref[0:HIST, :] = buf0_ref[0]

    x = x_ref[0]
    h = _rms(x, g_ref[...])
    proj_ref[...] = _dot(h, win_ref[...])
    hist_ref[HIST:HIST + tm, :] = proj_ref[:, 0:D_MODEL] * jax.nn.sigmoid(proj_ref[:, D_MODEL:2 * D_MODEL])

    rc = min(tm, 64)
    for r in range(tm // rc):
        for lb in range(D_MODEL // LANES):
            ls = slice(lb * LANES, (lb + 1) * LANES)
            c_ref[r * rc:(r + 1) * rc, ls] = _dwconv_block(hist_ref, wdw_ref, bdw_ref, r * rc, rc, ls, CONV_W, HIST)

    rc = min(tm, 16)
    for r in range(tm // rc):
        acc = c_ref[r * rc:(r + 1) * rc, :]
        mu = jnp.mean(acc, axis=-1, keepdims=True)
        d = acc - mu
        var = jnp.mean(d * d, axis=-1, keepdims=True)
        c = d * lax.rsqrt(var + LN_EPS) * lng_ref[...] + lnb_ref[...]
        gate = proj_ref[r * rc:(r + 1) * rc, 2 * D_MODEL:3 * D_MODEL]
        y_ref[r * rc:(r + 1) * rc, :] = (_silu(c) * _silu(gate)).astype(BF16)

    xo_ref[0] = x + jnp.dot(y_ref[...], wout_ref[...], preferred_element_type=F32)

    if tm >= HIST:
        hist_ref[0:HIST, :] = hist_ref[tm:tm + HIST, :]
    else:
        hist_ref[0:HIST - tm, :] = hist_ref[tm:HIST, :]
        hist_ref[HIST - tm:HIST, :] = hist_ref[HIST:HIST + tm, :]
    bufo_ref[0] = hist_ref[0:HIST, :]


def _layer_a(x, buf, g, w_in, w_dw, b_dw, ln_g, ln_b, w_out, *, tm):
    B, T, D = x.shape
    buf32 = jnp.concatenate([jnp.zeros((B, HIST - (CONV_W - 1), D), F32), buf], axis=1)
    row = lambda v: v.reshape(1, -1)
    xo, bufo = pl.pallas_call(
        functools.partial(_layer_a_kernel, tm=tm),
        grid=(B, T // tm),
        in_specs=[
            pl.BlockSpec((1, tm, D), lambda b, t: (b, t, 0)),
            pl.BlockSpec((1, HIST, D), lambda b, t: (b, 0, 0)),
            _const_spec((1, D)),
            _const_spec((D, 3 * D)),
            _const_spec((CONV_W, D)),
            _const_spec((1, D)),
            _const_spec((1, D)),
            _const_spec((1, D)),
            _const_spec((D, D)),
        ],
        out_specs=[
            pl.BlockSpec((1, tm, D), lambda b, t: (b, t, 0)),
            pl.BlockSpec((1, HIST, D), lambda b, t: (b, 0, 0)),
        ],
        out_shape=[jax.ShapeDtypeStruct((B, T, D), F32), jax.ShapeDtypeStruct((B, HIST, D), F32)],
        scratch_shapes=[
            pltpu.VMEM((tm, 3 * D), F32),
            pltpu.VMEM((HIST + tm, D), F32),
            pltpu.VMEM((tm, D), F32),
            pltpu.VMEM((tm, D), BF16),
        ],
        compiler_params=pltpu.CompilerParams(
            dimension_semantics=("arbitrary", "arbitrary"), vmem_limit_bytes=VMEM_LIMIT),
        name="layer_a",
    )(x, buf32, row(g), w_in, w_dw, row(b_dw), row(ln_g), row(ln_b), w_out)
    return xo, bufo[:, HIST - (CONV_W - 1):]


D_RNN = 1280
LRU_BLOCKS = 10
LRU_BLOCK = 128
LRU_CONV_W = 4
LRU_C = 8.0
BHIST = 8


def _scan_affine_rows(a, b):
    n = a.shape[0]
    rows = lax.broadcasted_iota(jnp.int32, a.shape, 0)
    k = 1
    while k < n:
        a_s = jnp.where(rows >= k, pltpu.roll(a, k, 0), 1.0)
        b_s = jnp.where(rows >= k, pltpu.roll(b, k, 0), 0.0)
        b = a * b_s + b
        a = a * a_s
        k *= 2
    return a, b


def _one_minus_exp(z, ez):
    s = 1.0 / 120.0
    for c in (1.0 / 24.0, 1.0 / 6.0, 0.5, 1.0):
        s = c + z * s
    return jnp.where(z > -1.0 / 32.0, -z * s, 1.0 - ez)


def _sqrt_nonneg(x):
    return jnp.where(x > 0.0, x * lax.rsqrt(x), 0.0)


def _layer_b_kernel(x_ref, buf0_ref, h0_ref, g_ref, win_ref, wc_ref, bc_ref, wg_ref, brg_ref, big_ref, lam_ref,
                    wout_ref, xo_ref, bufo_ref, ho_ref, proj_ref, hist_ref, carry_ref, y_ref, *, tm):
    ti = pl.program_id(1)

    @pl.when(ti == 0)
    def _():
        hist_ref[0:BHIST, :] = buf0_ref[0]
        carry_ref[...] = h0_ref[0]

    x = x_ref[0]
    h = _rms(x, g_ref[...])
    proj_ref[...] = _dot(h, win_ref[...])
    hist_ref[BHIST:BHIST + tm, :] = proj_ref[:, 0:D_RNN]

    lam = lam_ref[...]
    neg_c_sp = -LRU_C * (jnp.maximum(-lam, 0.0) + jnp.log1p(jnp.exp(-jnp.abs(lam))))

    rc = min(tm, 64)
    for r in range(tm // rc):
        r0 = r * rc
        xc = jnp.concatenate(
            [_dwconv_block(hist_ref, wc_ref, bc_ref, r0, rc, slice(lb * LANES, (lb + 1) * LANES), LRU_CONV_W, BHIST)
             for lb in range(D_RNN // LANES)], axis=1)
        xcb = xc.astype(BF16)
        rg, ig = [], []
        for n in range(LRU_BLOCKS):
            gg = jnp.dot(xcb[:, n * LRU_BLOCK:(n + 1) * LRU_BLOCK], wg_ref[n], preferred_element_type=F32)
            rg.append(gg[:, 0:LRU_BLOCK])
            ig.append(gg[:, LRU_BLOCK:2 * LRU_BLOCK])
        rgate = jax.nn.sigmoid(jnp.concatenate(rg, axis=1) + brg_ref[...])
        igate = jax.nn.sigmoid(jnp.concatenate(ig, axis=1) + big_ref[...])
        log_a = neg_c_sp * rgate
        a = jnp.exp(log_a)
        bx = _sqrt_nonneg(_one_minus_exp(2.0 * log_a, a * a)) * (igate * xc)
        a_cum, b_cum = _scan_affine_rows(a, bx)
        hs = a_cum * carry_ref[...] + b_cum
        carry_ref[...] = hs[rc - 1:rc, :]
        gate = proj_ref[r0:r0 + rc, D_RNN:2 * D_RNN]
        y_ref[r0:r0 + rc, :] = (hs * _silu(gate)).astype(BF16)

    xo_ref[0] = x + jnp.dot(y_ref[...], wout_ref[...], preferred_element_type=F32)
    hist_ref[0:BHIST, :] = hist_ref[tm:tm + BHIST, :]
    bufo_ref[0] = hist_ref[0:BHIST, :]
    ho_ref[0] = carry_ref[...]


def _layer_b(x, buf, h0, g, w_in, w_conv, b_conv, w_g, b_rg, b_ig, lam, w_out, *, tm):
    B, T, D = x.shape
    buf8 = jnp.concatenate([jnp.zeros((B, BHIST - (LRU_CONV_W - 1), D_RNN), F32), buf], axis=1)
    row = lambda v: v.reshape(1, -1)
    xo, bufo, ho = pl.pallas_call(
        functools.partial(_layer_b_kernel, tm=tm),
        grid=(B, T // tm),
        in_specs=[
            pl.BlockSpec((1, tm, D), lambda b, t: (b, t, 0)),
            pl.BlockSpec((1, BHIST, D_RNN), lambda b, t: (b, 0, 0)),
            pl.BlockSpec((1, 1, D_RNN), lambda b, t: (b, 0, 0)),
            _const_spec((1, D)),
            _const_spec((D, 2 * D_RNN)),
            _const_spec((LRU_CONV_W, D_RNN)),
            _const_spec((1, D_RNN)),
            _const_spec((LRU_BLOCKS, LRU_BLOCK, 2 * LRU_BLOCK)),
            _const_spec((1, D_RNN)),
            _const_spec((1, D_RNN)),
            _const_spec((1, D_RNN)),
            _const_spec((D_RNN, D)),
        ],
        out_specs=[
            pl.BlockSpec((1, tm, D), lambda b, t: (b, t, 0)),
            pl.BlockSpec((1, BHIST, D_RNN), lambda b, t: (b, 0, 0)),
            pl.BlockSpec((1, 1, D_RNN), lambda b, t: (b, 0, 0)),
        ],
        out_shape=[jax.ShapeDtypeStruct((B, T, D), F32), jax.ShapeDtypeStruct((B, BHIST, D_RNN), F32),
                   jax.ShapeDtypeStruct((B, 1, D_RNN), F32)],
        scratch_shapes=[
            pltpu.VMEM((tm, 2 * D_RNN), F32),
            pltpu.VMEM((BHIST + tm, D_RNN), F32),
            pltpu.VMEM((1, D_RNN), F32),
            pltpu.VMEM((tm, D_RNN), BF16),
        ],
        compiler_params=pltpu.CompilerParams(
            dimension_semantics=("arbitrary", "arbitrary"), vmem_limit_bytes=VMEM_LIMIT),
        name="layer_b",
    )(x, buf8, h0.reshape(B, 1, D_RNN), row(g), w_in, w_conv, row(b_conv), w_g, row(b_rg), row(b_ig), row(lam), w_out)
    return xo, bufo[:, BHIST - (LRU_CONV_W - 1):], ho.reshape(B, D_RNN)


RET_HEADS = 4
RET_DK = 256
RET_DV = 512
D_RET_V = RET_HEADS * RET_DV
ROPE_BASE = 10000.0


def _rot(z, cos, sin):
    half = RET_DK // 2
    z1, z2 = z[:, 0:half], z[:, half:RET_DK]
    return jnp.concatenate([z1 * cos - z2 * sin, z1 * sin + z2 * cos], axis=1)


def _layer_d_kernel(x_ref, s0_ref, cos_ref, sin_ref, g_ref, win_ref, gng_ref, wout_ref, fg_ref,
                    xo_ref, so_ref, proj_ref, y_ref, *, tm, lgs):
    ti = pl.program_id(1)

    @pl.when(ti == 0)
    def _():
        so_ref[...] = s0_ref[...]

    x = x_ref[0]
    h = _rms(x, g_ref[...])
    proj_ref[...] = _dot(h, win_ref[...])
    cos = cos_ref[...]
    sin = sin_ref[...]
    row = lax.broadcasted_iota(jnp.int32, (tm, tm), 0)
    col = lax.broadcasted_iota(jnp.int32, (tm, tm), 1)
    causal = row >= col
    diff = jnp.where(causal, row - col, 0).astype(F32)
    idx = lax.broadcasted_iota(jnp.int32, (tm, 1), 0).astype(F32)

    for hh in range(RET_HEADS):
        lg = lgs[hh]
        q = _rot(proj_ref[:, hh * RET_DK:(hh + 1) * RET_DK], cos, sin) * (RET_DK ** -0.5)
        k = _rot(proj_ref[:, D_MODEL + hh * RET_DK:D_MODEL + (hh + 1) * RET_DK], cos, sin)
        v = proj_ref[:, 2 * D_MODEL + hh * RET_DV:2 * D_MODEL + (hh + 1) * RET_DV].astype(BF16)
        qb = q.astype(BF16)
        dmask = jnp.where(causal, jnp.exp(diff * lg), 0.0)
        scores = lax.dot_general(qb, k.astype(BF16), (((1,), (1,)), ((), ())), preferred_element_type=F32) * dmask
        s_old = so_ref[0, hh]
        o = jnp.dot(scores.astype(BF16), v, preferred_element_type=F32)
        o = o + jnp.dot(qb, s_old.astype(BF16), preferred_element_type=F32) * jnp.exp((idx + 1.0) * lg)
        k_dec = (k * jnp.exp((tm - 1.0 - idx) * lg)).astype(BF16)
        so_ref[0, hh] = s_old * jnp.exp(jnp.full((1, 1), tm * lg, F32)) + lax.dot_general(
            k_dec, v, (((0,), (0,)), ((), ())), preferred_element_type=F32)
        o = o * lax.rsqrt(jnp.mean(o * o, axis=-1, keepdims=True) + NORM_EPS)
        o = o * gng_ref[:, hh * RET_DV:(hh + 1) * RET_DV]
        gate = proj_ref[:, 2 * D_MODEL + D_RET_V + hh * RET_DV:2 * D_MODEL + D_RET_V + (hh + 1) * RET_DV]
        y_ref[:, hh * RET_DV:(hh + 1) * RET_DV] = (o * _silu(gate)).astype(BF16)

    xn = x + jnp.dot(y_ref[...], wout_ref[...], preferred_element_type=F32)
    xo_ref[0] = _rms(xn, fg_ref[...])


def _layer_d(x, s0, pos0, g, w_in, gn_g, w_out, final_g, *, tm):
    B, T, D = x.shape
    lgs =tuple(math.log1p(-(2.0 ** (-5.0 - hh))) for hh in range(RET_HEADS))
    half = RET_DK // 2
    inv = ROPE_BASE ** (-jnp.arange(half, dtype=F32) / half)
    ang = (pos0 + jnp.arange(T)).astype(F32)[:, None] * inv[None]
    cos, sin = jnp.cos(ang), jnp.sin(ang)
    row = lambda v: v.reshape(1, -1)
    nproj = 2 * D + 2 * D_RET_V
    xo, so = pl.pallas_call(
        functools.partial(_layer_d_kernel, tm=tm, lgs=lgs),
        grid=(B, T // tm),
        in_specs=[
            pl.BlockSpec((1, tm, D), lambda b, t: (b, t, 0)),
            pl.BlockSpec((1, RET_HEADS, RET_DK, RET_DV), lambda b, t: (b, 0, 0, 0)),
            pl.BlockSpec((tm, half), lambda b, t: (t, 0)),
            pl.BlockSpec((tm, half), lambda b, t: (t, 0)),
            _const_spec((1, D)),
            _const_spec((D, nproj)),
            _const_spec((1, D_RET_V)),
            _const_spec((D_RET_V, D)),
            _const_spec((1, D)),
        ],
        out_specs=[
            pl.BlockSpec((1, tm, D), lambda b, t: (b, t, 0)),
            pl.BlockSpec((1, RET_HEADS, RET_DK, RET_DV), lambda b, t: (b, 0, 0, 0)),
        ],
        out_shape=[jax.ShapeDtypeStruct((B, T, D), F32),
                   jax.ShapeDtypeStruct((B, RET_HEADS, RET_DK, RET_DV), F32)],
        scratch_shapes=[
            pltpu.VMEM((tm, nproj), F32),
            pltpu.VMEM((tm, D_RET_V), BF16),
        ],
        compiler_params=pltpu.CompilerParams(
            dimension_semantics=("arbitrary", "arbitrary"), vmem_limit_bytes=VMEM_LIMIT),
        name="layer_d",
    )(x, s0, cos, sin, row(g), w_in, row(gn_g), w_out, row(final_g))
    return xo, so


RWKV_HEAD = 64
RWKV_HEADS = 16
D_LORA = 64
RWKV_GN_EPS = 64e-5


def _seg_mats():
    c = jnp.arange(D_MODEL)[:, None] // RWKV_HEAD
    hcol = jnp.arange(LANES)[None, :]
    seg = (c == hcol).astype(BF16)
    return seg, seg.T


def _segsum(z, seg):
    return jnp.dot(z.astype(BF16), seg, preferred_element_type=F32)


def _segbcast(s, bc):
    hi = s.astype(BF16)
    lo = (s - hi.astype(F32)).astype(BF16)
    return jnp.dot(hi, bc, preferred_element_type=F32) + jnp.dot(lo, bc, preferred_element_type=F32)


def _layer_c1_kernel(x_ref, sh0_ref, g_ref, mu_ref, win_ref, w0_ref, w1_ref, w2_ref, a0_ref, a1_ref, a2_ref,
                     ka_ref, rk_ref, seg_ref, bc_ref,
                     r_ref, w_ref, k_ref, v_ref, a_ref, gt_ref, bonus_ref, sho_ref, carry_ref, *, tm):
    ti = pl.program_id(1)

    @pl.when(ti == 0)
    def _():
        carry_ref[...] = sh0_ref[0]

    h = _rms(x_ref[0], g_ref[...])
    rows = lax.broadcasted_iota(jnp.int32, h.shape, 0)
    h_prev = jnp.where(rows == 0, carry_ref[...], pltpu.roll(h, 1, 0))
    carry_ref[...] = h[tm - 1:tm, :]
    sho_ref[0] = h[tm - 1:tm, :]
    xx = h_prev - h

    def mix(s):
        return (h + xx * mu_ref[s:s + 1, :]).astype(BF16)

    r = jnp.dot(mix(0), win_ref[:, 0:D_MODEL], preferred_element_type=F32)
    k = jnp.dot(mix(1), win_ref[:, D_MODEL:2 * D_MODEL], preferred_element_type=F32)
    v = jnp.dot(mix(2), win_ref[:, 2 * D_MODEL:3 * D_MODEL], preferred_element_type=F32)
    gt_ref[0] = jnp.dot(mix(3), win_ref[:, 3 * D_MODEL:4 * D_MODEL], preferred_element_type=F32)
    r_ref[0] = r
    v_ref[0] = v

    wl = jnp.tanh(jnp.dot(mix(4), w1_ref[...], preferred_element_type=F32))
    wx = w0_ref[...] + _dot(wl, w2_ref[...])
    w_log = -(jnp.maximum(-wx, 0.0) + jnp.log1p(jnp.exp(-jnp.abs(wx)))) - 0.5
    w_ref[0] = jnp.exp(-jnp.exp(w_log))

    al = jnp.dot(mix(5), a1_ref[...], preferred_element_type=F32)
    a = jax.nn.sigmoid(a0_ref[...] + _dot(al, a2_ref[...]))

    k_ref[0] = k
    a_ref[0] = a
    kh = k * (1.0 + (a - 1.0) * ka_ref[...])
    rk = _segsum(r * kh * rk_ref[...], seg_ref[...])
    bonus_ref[0] = _segbcast(rk, bc_ref[...]) * v


def _layer_c1(x, sh0, g, mu, w_in, w0, w1, w2, a0, a1, a2, k_a, r_k, *, tm):
    B, T, D = x.shape
    seg, bc = _seg_mats()
    row = lambda v: v.reshape(1, -1)
    tile = pl.BlockSpec((1, tm, D), lambda b, t: (b, t, 0))
    outs = pl.pallas_call(
        functools.partial(_layer_c1_kernel, tm=tm),
        grid=(B, T // tm),
        in_specs=[
            tile,
            pl.BlockSpec((1, 1, D), lambda b, t: (b, 0, 0)),
            _const_spec((1, D)),
            _const_spec((6, D)),
            _const_spec((D, 4 * D)),
            _const_spec((1, D)),
            _const_spec((D, D_LORA)),
            _const_spec((D_LORA, D)),
            _const_spec((1, D)),
            _const_spec((D, D_LORA)),
            _const_spec((D_LORA, D)),
            _const_spec((1, D)),
            _const_spec((1, D)),
            _const_spec((D, LANES)),
            _const_spec((LANES, D)),
        ],
        out_specs=[tile] * 7 + [pl.BlockSpec((1, 1, D), lambda b, t: (b, 0, 0))],
        out_shape=[jax.ShapeDtypeStruct((B, T, D), F32)] * 7 + [jax.ShapeDtypeStruct((B, 1, D), F32)],
        scratch_shapes=[pltpu.VMEM((1, D), F32)],
        compiler_params=pltpu.CompilerParams(
            dimension_semantics=("arbitrary", "arbitrary"), vmem_limit_bytes=VMEM_LIMIT),
        name="layer_c1",
    )(x, sh0.reshape(B, 1, D), row(g), mu, w_in, row(w0), w1, w2, row(a0), a1, a2, row(k_a), row(r_k), seg, bc)
    return outs[:7], outs[7].reshape(B, D)


def _layer_c3_kernel(x_ref, o_ref, bonus_ref, gt_ref, gng_ref, gnb_ref, wout_ref, seg_ref, bc_ref, xo_ref):
    o = o_ref[0]
    mu = _segsum(o, seg_ref[...]) * (1.0 / RWKV_HEAD)
    d = o - _segbcast(mu, bc_ref[...])
    var = _segsum(d * d, seg_ref[...]) * (1.0 / RWKV_HEAD)
    o = d * _segbcast(lax.rsqrt(var + RWKV_GN_EPS), bc_ref[...]) * gng_ref[...] + gnb_ref[...]
    y = (o + bonus_ref[0]) * _silu(gt_ref[0])
    xo_ref[0] = x_ref[0] + _dot(y, wout_ref[...])


def _layer_c3(x, o, bonus, gt, gn_g, gn_b, w_out, *, tm):
    B, T, D = x.shape
    seg, bc = _seg_mats()
    row = lambda v: v.reshape(1, -1)
    tile = pl.BlockSpec((1, tm, D), lambda b, t: (b, t, 0))
    return pl.pallas_call(
        _layer_c3_kernel,
        grid=(B, T // tm),
        in_specs=[tile, tile, tile, tile, _const_spec((1, D)), _const_spec((1, D)), _const_spec((D, D)),
                  _const_spec((D, LANES)), _const_spec((LANES, D))],
        out_specs=tile,
        out_shape=jax.ShapeDtypeStruct((B, T, D), F32),
        compiler_params=pltpu.CompilerParams(
            dimension_semantics=("arbitrary", "arbitrary"), vmem_limit_bytes=VMEM_LIMIT),
        name="layer_c3",
    )(x, o, bonus, gt, row(gn_g), row(gn_b), w_out, seg, bc)


def _wkv_kernel(r_ref, w_ref, k_ref, a_ref, v_ref, kkt_ref, kat_ref, s0_ref, y_ref, so_ref,
                col_ref, vt_ref, yt_ref, *, tq, nb, packed):
    @pl.when(pl.program_id(0) == 0)
    def _():
        so_ref[...] = s0_ref[...]

    half = LANES // 2
    npar = 2 if packed else 1
    nrow = RWKV_HEAD // npar
    low = lax.broadcasted_iota(jnp.int32, (RWKV_HEAD, LANES), 1) < half
    low_r = lax.broadcasted_iota(jnp.int32, (nrow, LANES), 1) < half
    Q_R, Q_W, Q_K, Q_A, Q_B, Q_V = range(6)
    nblk = tq // SUBLANES

    def relayout(blk):
        t8 = pl.multiple_of(blk * SUBLANES, SUBLANES)
        for q, ref in ((Q_R, r_ref), (Q_W, w_ref), (Q_K, k_ref), (Q_A, a_ref), (Q_V, v_ref)):
            slabs = [ref[b, pl.ds(t8, SUBLANES), :].reshape(SUBLANES, RWKV_HEADS, RWKV_HEAD) for b in range(nb)]
            for g in range(SUBLANES // npar):
                mt = jnp.concatenate([slabs[b][npar * g + par] for par in range(npar) for b in range(nb)], axis=0).T
                t0 = t8 + npar * g
                if not packed:
                    if q == Q_V:
                        vt_ref[t0] = mt
                    else:
                        col_ref[t0, q] = mt
                elif q == Q_V:
                    top, bot = mt[0:nrow], mt[nrow:RWKV_HEAD]
                    vt_ref[t0] = jnp.where(low_r, top, pltpu.roll(bot, half, 1))
                    vt_ref[t0 + 1] = jnp.where(low_r, pltpu.roll(top, half, 1), bot)
                else:
                    rolled = pltpu.roll(mt, half, 1)
                    col_ref[t0, q] = jnp.where(low, mt, rolled)
                    col_ref[t0 + 1, q] = jnp.where(low, rolled, mt)
        for i in range(SUBLANES):
            t = t8 + i
            k, a = col_ref[t, Q_K], col_ref[t, Q_A]
            kk = k * kkt_ref[...]
            n2 = jnp.sum(kk * kk, axis=0, keepdims=True)
            kk = kk * jnp.minimum(lax.rsqrt(n2), 1e12)
            col_ref[t, Q_K] = k * (1.0 + (a - 1.0) * kat_ref[...])
            col_ref[t, Q_A] = -kk
            col_ref[t, Q_B] = kk * a

    def writeback(blk):
        t8 = pl.multiple_of(blk * SUBLANES, SUBLANES)
        slabs = [[None] * SUBLANES for _ in range(nb)]
        for g in range(SUBLANES // npar):
            t0 = t8 + npar * g
            if packed:
                y0, y1 = yt_ref[t0], yt_ref[t0 + 1]
                yt = jnp.concatenate([jnp.where(low_r, y0, pltpu.roll(y1, half, 1)),
                                      jnp.where(low_r, pltpu.roll(y0, half, 1), y1)], axis=0).T
            else:
                yt = yt_ref[t0].T
            for par in range(npar):
                for b in range(nb):
                    r0 = (par * nb + b) * RWKV_HEADS
                    slabs[b][npar * g + par] = yt[r0:r0 + RWKV_HEADS, :]
        for b in range(nb):
            y_ref[b, pl.ds(t8, SUBLANES), :] = jnp.stack(slabs[b], axis=0).reshape(SUBLANES, D_MODEL)

    def token(t, sa):
        tn = jnp.minimum(t + 1, tq - 1)
        vt = vt_ref[t]
        y, nxt = [None, None], [None, None]
        for j in range(RWKV_HEAD):
            row = lambda q: col_ref[t, q, j:j + 1, :]
            s = so_ref[j] * row(Q_W) + sa * row(Q_B) + vt * row(Q_K)
            so_ref[j] = s
            yj, nj = s * row(Q_R), s * col_ref[tn, Q_A, j:j + 1, :]
            y[j % 2] = yj if y[j % 2] is None else y[j % 2] + yj
            nxt[j % 2] = nj if nxt[j % 2] is None else nxt[j % 2] + nj
        yt_ref[t] = y[0] + y[1]
        return nxt[0] + nxt[1]

    def relayout_blocks(u, carry):
        for n in range(2):
            relayout(2 * u + n)
        return carry

    def writeback_blocks(u, carry):
        for n in range(2):
            writeback(2 * u + n)
        return carry

    lax.fori_loop(0, nblk // 2, relayout_blocks, 0)

    acc = [so_ref[0] * col_ref[0, Q_A, 0:1, :], so_ref[1] * col_ref[0, Q_A, 1:2, :]]
    for j in range(2, RWKV_HEAD):
        acc[j % 2] = acc[j % 2] + so_ref[j] * col_ref[0, Q_A, j:j + 1, :]
    lax.fori_loop(0, tq, token, acc[0] + acc[1])

    lax.fori_loop(0, nblk // 2, writeback_blocks, 0)


def _wkv_layer(r, w, k, v, a, k_k, k_a, s0):
    B, T, D = r.shape
    H, N = RWKV_HEADS, RWKV_HEAD
    packed = B * H == LANES // 2
    assert packed or B * H == LANES
    if packed:
        ni = N // 2
        st_in = s0.reshape(B, H, 2, ni, N).transpose(4, 3, 2, 0, 1).reshape(N, ni, LANES)
        st_out = lambda s: s.reshape(N, ni, 2, B, H).transpose(3, 4, 2, 1, 0).reshape(B, H, N, N)
    else:
        ni = N
        st_in = s0.transpose(3, 2, 0, 1).reshape(N, N, LANES)
        st_out = lambda s: s.reshape(N, N, B, H).transpose(2, 3, 1, 0)
    tq = min(T, 64)
    tok_spec = pl.BlockSpec((B, tq, D), lambda t: (0, t, 0))
    st_spec = pl.BlockSpec((N, ni, LANES), lambda t: (0, 0, 0))
    par_spec = pl.BlockSpec((N, LANES), lambda t: (0, 0))
    lane_tile = lambda p: jnp.tile(p.reshape(H, N).T, (1, LANES // H))
    y, s = pl.pallas_call(
        functools.partial(_wkv_kernel, tq=tq, nb=B, packed=packed),
        grid=(T // tq,),
        in_specs=[tok_spec] * 5 + [par_spec, par_spec, st_spec],
        out_specs=[tok_spec, st_spec],
        out_shape=[jax.ShapeDtypeStruct((B, T, D), F32), jax.ShapeDtypeStruct((N, ni, LANES), F32)],
        scratch_shapes=[pltpu.VMEM((tq, 5, N, LANES), F32), pltpu.VMEM((tq, ni, LANES), F32),
                        pltpu.VMEM((tq, ni, LANES), F32)],
        compiler_params=pltpu.CompilerParams(dimension_semantics=("arbitrary",), vmem_limit_bytes=VMEM_LIMIT),
        name="wkv",
    )(r, w, k, a, v, lane_tile(k_k), lane_tile(k_a), st_in)
    return y, st_out(s)


def kernel(x_prompt, x_sample, cache_conv_a, cache_conv_b, state_lru_b, state_shift_c, state_wkv_c, state_ret_d, meta_tokens, norm_g, final_norm_g, a_w_in, a_w_dw, a_b_dw, a_ln_g, a_ln_b, a_w_out, b_w_in, b_w_conv, b_b_conv, b_w_rg, b_b_rg, b_w_ig, b_b_ig, b_lam, b_w_out, c_mu, c_w_in, c_w0, c_w1, c_w2, c_a0, c_a1, c_a2, c_k_k, c_k_a, c_r_k, c_gn_g, c_gn_b, c_w_out, d_w_in, d_gn_g, d_w_out):
    bf = lambda z: z.astype(BF16)
    a_win, a_wout = bf(a_w_in), bf(a_w_out)
    b_win, b_wout = bf(b_w_in), bf(b_w_out)
    b_wg = bf(jnp.concatenate([b_w_rg, b_w_ig], axis=-1))
    c_win, c_wout = bf(c_w_in), bf(c_w_out)
    c_w1b, c_w2b, c_a1b, c_a2b = bf(c_w1), bf(c_w2), bf(c_a1), bf(c_a2)
    d_win, d_wout = bf(d_w_in), bf(d_w_out)

    def trunk(x, st, pos0, tm):
        x, conv_a = _layer_a(x, st[0], norm_g[0], a_win, a_w_dw, a_b_dw, a_ln_g, a_ln_b, a_wout, tm=tm)
        x, conv_b, lru = _layer_b(x, st[1], st[2], norm_g[1], b_win, b_w_conv, b_b_conv, b_wg, b_b_rg, b_b_ig,
                                  b_lam, b_wout, tm=tm)
        (r, w, k, v, a, gt, bonus), shift = _layer_c1(
            x, st[3], norm_g[2], c_mu, c_win, c_w0, c_w1b, c_w2b, c_a0, c_a1b, c_a2b, c_k_a,
            c_r_k.reshape(-1), tm=tm)
        o, wkv = _wkv_layer(r, w, k, v, a, c_k_k, c_k_a, st[4])
        x = _layer_c3(x, o, bonus, gt, c_gn_g, c_gn_b, c_wout, tm=tm)
        x, ret = _layer_d(x, st[5], pos0, norm_g[3], d_win, d_gn_g, d_wout, final_norm_g, tm=tm)
        return x, (conv_a, conv_b, lru, shift, wkv, ret)

    B = x_prompt.shape[0]
    zeros = (jnp.zeros((B, CONV_W - 1, D_MODEL), F32), jnp.zeros((B, LRU_CONV_W - 1, D_RNN), F32),
             jnp.zeros((B, D_RNN), F32), jnp.zeros((B, D_MODEL), F32),
             jnp.zeros((B, RWKV_HEADS, RWKV_HEAD, RWKV_HEAD), F32), jnp.zeros((B, RET_HEADS, RET_DK, RET_DV), F32))
    meta = jnp.broadcast_to(meta_tokens[None], (B, N_META, D_MODEL))
    _, st_meta = trunk(meta, zeros, 0, N_META)
    y_prompt, stp = trunk(x_prompt, st_meta, N_META, 256)
    st_s = (cache_conv_a, cache_conv_b, state_lru_b, state_shift_c, state_wkv_c, state_ret_d)
    y_sample, sts = trunk(x_sample, st_s, N_META + PAST_LEN, x_sample.shape[1])
    return (y_prompt, y_sample, stp[0], sts[0], stp[1], sts[1], stp[2], sts[2], stp[3], sts[3],
            stp[4], sts[4], stp[5], sts[5])
```

```python
import functools
import math

import jax
import jax.numpy as jnp
from jax import lax
from jax.experimental import pallas as pl
from jax.experimental.pallas import tpu as pltpu

F32 = jnp.float32
BF16 = jnp.bfloat16

D_MODEL = 1024
N_META = 16
PAST_LEN = 1024
NORM_EPS = 1e-6
LN_EPS = 1e-5
CONV_W = 31
HIST = 32
LANES = 128
SUBLANES = 8

VMEM_LIMIT = 56 * 1024 * 1024


def _rms(x, g):
    return x * lax.rsqrt(jnp.mean(x * x, axis=-1, keepdims=True) + NORM_EPS) * g


def _silu(x):
    return x * jax.nn.sigmoid(x)


def _dot(a, b):
    return jnp.dot(a.astype(BF16), b, preferred_element_type=F32)


def _dwconv_block(hist_ref, w_ref, b_ref, r0, rc, ls, width, nhist):
    off = nhist - (width - 1)
    nwin = rc + nhist
    win = hist_ref[r0:r0 + nwin, ls]
    acc = jnp.zeros((rc, ls.stop - ls.start), F32) + b_ref[:, ls]
    for res in range(SUBLANES):
        taps = [s for s in range(res, nhist + 1, SUBLANES) if 0 <= s - off < width]
        if not taps:
            continue
        wr = win if res == 0 else pltpu.roll(win, nwin - res, 0)
        for s in taps:
            acc = acc + w_ref[s - off:s - off + 1, ls] * wr[s - res:s - res + rc]
    return acc


def _const_spec(shape):
    n = len(shape)
    return pl.BlockSpec(shape, lambda b, t: (0,) * n, pipeline_mode=pl.Buffered(1))


def _layer_a_kernel(x_ref, buf0_ref, g_ref, win_ref, wdw_ref, bdw_ref, lng_ref, lnb_ref, wout_ref,
                    xo_ref, bufo_ref, proj_ref, hist_ref, c_ref, y_ref, *, tm):
    ti = pl.program_id(1)

    @pl.when(ti == 0)
    def _():
        hist_ref[0:HIST, :] = buf0_ref[0]

    x = x_ref[0]
    h = _rms(x, g_ref[...])
    proj_ref[...] = _dot(h, win_ref[...])
    hist_ref[HIST:HIST + tm, :] = proj_ref[:, 0:D_MODEL] * jax.nn.sigmoid(proj_ref[:, D_MODEL:2 * D_MODEL])

    rc = min(tm, 64)
    for r in range(tm // rc):
        for lb in range(D_MODEL // LANES):
            ls = slice(lb * LANES, (lb + 1) * LANES)
            c_ref[r * rc:(r + 1) * rc, ls] = _dwconv_block(hist_ref, wdw_ref, bdw_ref, r * rc, rc, ls, CONV_W, HIST)

    rc = min(tm, 16)
    for r in range(tm // rc):
        acc = c_ref[r * rc:(r + 1) * rc, :]
        mu = jnp.mean(acc, axis=-1, keepdims=True)
        d = acc - mu
        var = jnp.mean(d * d, axis=-1, keepdims=True)
        c = d * lax.rsqrt(var + LN_EPS) * lng_ref[...] + lnb_ref[...]
        gate = proj_ref[r * rc:(r + 1) * rc, 2 * D_MODEL:3 * D_MODEL]
        y_ref[r * rc:(r + 1) * rc, :] = (_silu(c) * _silu(gate)).astype(BF16)

    xo_ref[0] = x + jnp.dot(y_ref[...], wout_ref[...], preferred_element_type=F32)

    if tm >= HIST:
        hist_ref[0:HIST, :] = hist_ref[tm:tm + HIST, :]
    else:
        hist_ref[0:HIST - tm, :] = hist_ref[tm:HIST, :]
        hist_ref[HIST - tm:HIST, :] = hist_ref[HIST:HIST + tm, :]
    bufo_ref[0] = hist_ref[0:HIST, :]


def _layer_a(x, buf, g, w_in, w_dw, b_dw, ln_g, ln_b, w_out, *, tm):
    B, T, D = x.shape
    buf32 = jnp.concatenate([jnp.zeros((B, HIST - (CONV_W - 1), D), F32), buf], axis=1)
    row = lambda v: v.reshape(1, -1)
    xo, bufo = pl.pallas_call(
        functools.partial(_layer_a_kernel, tm=tm),
        grid=(B, T // tm),
        in_specs=[
            pl.BlockSpec((1, tm, D), lambda b, t: (b, t, 0)),
            pl.BlockSpec((1, HIST, D), lambda b, t: (b, 0, 0)),
            _const_spec((1, D)),
            _const_spec((D, 3 * D)),
            _const_spec((CONV_W, D)),
            _const_spec((1, D)),
            _const_spec((1, D)),
            _const_spec((1, D)),
            _const_spec((D, D)),
        ],
        out_specs=[
            pl.BlockSpec((1, tm, D), lambda b, t: (b, t, 0)),
            pl.BlockSpec((1, HIST, D), lambda b, t: (b, 0, 0)),
        ],
        out_shape=[jax.ShapeDtypeStruct((B, T, D), F32), jax.ShapeDtypeStruct((B, HIST, D), F32)],
        scratch_shapes=[
            pltpu.VMEM((tm, 3 * D), F32),
            pltpu.VMEM((HIST + tm, D), F32),
            pltpu.VMEM((tm, D), F32),
            pltpu.VMEM((tm, D), BF16),
        ],
        compiler_params=pltpu.CompilerParams(
            dimension_semantics=("arbitrary", "arbitrary"), vmem_limit_bytes=VMEM_LIMIT),
        name="layer_a",
    )(x, buf32, row(g), w_in, w_dw, row(b_dw), row(ln_g), row(ln_b), w_out)
    return xo, bufo[:, HIST - (CONV_W - 1):]


D_RNN = 1280
LRU_BLOCKS = 10
LRU_BLOCK = 128
LRU_CONV_W = 4
LRU_C = 8.0
BHIST = 8


def _scan_affine_rows(a, b):
    n = a.shape[0]
    rows = lax.broadcasted_iota(jnp.int32, a.shape, 0)
    k = 1
    while k < n:
        a_s = jnp.where(rows >= k, pltpu.roll(a, k, 0), 1.0)
        b_s = jnp.where(rows >= k, pltpu.roll(b, k, 0), 0.0)
        b = a * b_s + b
        a = a * a_s
        k *= 2
    return a, b


def _one_minus_exp(z, ez):
    s = 1.0 / 120.0
    for c in (1.0 / 24.0, 1.0 / 6.0, 0.5, 1.0):
        s = c + z * s
    return jnp.where(z > -1.0 / 32.0, -z * s, 1.0 - ez)


def _sqrt_nonneg(x):
    return jnp.where(x > 0.0, x * lax.rsqrt(x), 0.0)


def _layer_b_kernel(x_ref, buf0_ref, h0_ref, g_ref, win_ref, wc_ref, bc_ref, wg_ref, brg_ref, big_ref, lam_ref,
                    wout_ref, xo_ref, bufo_ref, ho_ref, proj_ref, hist_ref, carry_ref, y_ref, *, tm):
    ti = pl.program_id(1)

    @pl.when(ti == 0)
    def _():
        hist_ref[0:BHIST, :] = buf0_ref[0]
        carry_ref[...] = h0_ref[0]

    x = x_ref[0]
    h = _rms(x, g_ref[...])
    proj_ref[...] = _dot(h, win_ref[...])
    hist_ref[BHIST:BHIST + tm, :] = proj_ref[:, 0:D_RNN]

    lam = lam_ref[...]
    neg_c_sp = -LRU_C * (jnp.maximum(-lam, 0.0) + jnp.log1p(jnp.exp(-jnp.abs(lam))))

    rc = min(tm, 64)
    for r in range(tm // rc):
        r0 = r * rc
        xc = jnp.concatenate(
            [_dwconv_block(hist_ref, wc_ref, bc_ref, r0, rc, slice(lb * LANES, (lb + 1) * LANES), LRU_CONV_W, BHIST)
             for lb in range(D_RNN // LANES)], axis=1)
        xcb = xc.astype(BF16)
        rg, ig = [], []
        for n in range(LRU_BLOCKS):
            gg = jnp.dot(xcb[:, n * LRU_BLOCK:(n + 1) * LRU_BLOCK], wg_ref[n], preferred_element_type=F32)
            rg.append(gg[:, 0:LRU_BLOCK])
            ig.append(gg[:, LRU_BLOCK:2 * LRU_BLOCK])
        rgate = jax.nn.sigmoid(jnp.concatenate(rg, axis=1) + brg_ref[...])
        igate = jax.nn.sigmoid(jnp.concatenate(ig, axis=1) + big_ref[...])
        log_a = neg_c_sp * rgate
        a = jnp.exp(log_a)
        bx = _sqrt_nonneg(_one_minus_exp(2.0 * log_a, a * a)) * (igate * xc)
        a_cum, b_cum = _scan_affine_rows(a, bx)
        hs = a_cum * carry_ref[...] + b_cum
        carry_ref[...] = hs[rc - 1:rc, :]
        gate = proj_ref[r0:r0 + rc, D_RNN:2 * D_RNN]
        y_ref[r0:r0 + rc, :] = (hs * _silu(gate)).astype(BF16)

    xo_ref[0] = x + jnp.dot(y_ref[...], wout_ref[...], preferred_element_type=F32)
    hist_ref[0:BHIST, :] = hist_ref[tm:tm + BHIST, :]
    bufo_ref[0] = hist_ref[0:BHIST, :]
    ho_ref[0] = carry_ref[...]


def _layer_b(x, buf, h0, g, w_in, w_conv, b_conv, w_g, b_rg, b_ig, lam, w_out, *, tm):
    B, T, D = x.shape
    buf8 = jnp.concatenate([jnp.zeros((B, BHIST - (LRU_CONV_W - 1), D_RNN), F32), buf], axis=1)
    row = lambda v: v.reshape(1, -1)
    xo, bufo, ho = pl.pallas_call(
        functools.partial(_layer_b_kernel, tm=tm),
        grid=(B, T // tm),
        in_specs=[
            pl.BlockSpec((1, tm, D), lambda b, t: (b, t, 0)),
            pl.BlockSpec((1, BHIST, D_RNN), lambda b, t: (b, 0, 0)),
            pl.BlockSpec((1, 1, D_RNN), lambda b, t: (b, 0, 0)),
            _const_spec((1, D)),
            _const_spec((D, 2 * D_RNN)),
            _const_spec((LRU_CONV_W, D_RNN)),
            _const_spec((1, D_RNN)),
            _const_spec((LRU_BLOCKS, LRU_BLOCK, 2 * LRU_BLOCK)),
            _const_spec((1, D_RNN)),
            _const_spec((1, D_RNN)),
            _const_spec((1, D_RNN)),
            _const_spec((D_RNN, D)),
        ],
        out_specs=[
            pl.BlockSpec((1, tm, D), lambda b, t: (b, t, 0)),
            pl.BlockSpec((1, BHIST, D_RNN), lambda b, t: (b, 0, 0)),
            pl.BlockSpec((1, 1, D_RNN), lambda b, t: (b, 0, 0)),
        ],
        out_shape=[jax.ShapeDtypeStruct((B, T, D), F32), jax.ShapeDtypeStruct((B, BHIST, D_RNN), F32),
                   jax.ShapeDtypeStruct((B, 1, D_RNN), F32)],
        scratch_shapes=[
            pltpu.VMEM((tm, 2 * D_RNN), F32),
            pltpu.VMEM((BHIST + tm, D_RNN), F32),
            pltpu.VMEM((1, D_RNN), F32),
            pltpu.VMEM((tm, D_RNN), BF16),
        ],
        compiler_params=pltpu.CompilerParams(
            dimension_semantics=("arbitrary", "arbitrary"), vmem_limit_bytes=VMEM_LIMIT),
        name="layer_b",
    )(x, buf8, h0.reshape(B, 1, D_RNN), row(g), w_in, w_conv, row(b_conv), w_g, row(b_rg), row(b_ig), row(lam), w_out)
    return xo, bufo[:, BHIST - (LRU_CONV_W - 1):], ho.reshape(B, D_RNN)


RET_HEADS = 4
RET_DK = 256
RET_DV = 512
D_RET_V = RET_HEADS * RET_DV
ROPE_BASE = 10000.0


def _rot(z, cos, sin):
    half = RET_DK // 2
    z1, z2 = z[:, 0:half], z[:, half:RET_DK]
    return jnp.concatenate([z1 * cos - z2 * sin, z1 * sin + z2 * cos], axis=1)


def _layer_d_kernel(x_ref, s0_ref, cos_ref, sin_ref, g_ref, win_ref, gng_ref, wout_ref, fg_ref,
                    xo_ref, so_ref, proj_ref, y_ref, *, tm, lgs):
    ti = pl.program_id(1)

    @pl.when(ti == 0)
    def _():
        so_ref[...] = s0_ref[...]

    x = x_ref[0]
    h = _rms(x, g_ref[...])
    proj_ref[...] = _dot(h, win_ref[...])
    cos = cos_ref[...]
    sin = sin_ref[...]
    row = lax.broadcasted_iota(jnp.int32, (tm, tm), 0)
    col = lax.broadcasted_iota(jnp.int32, (tm, tm), 1)
    causal = row >= col
    diff = jnp.where(causal, row - col, 0).astype(F32)
    idx = lax.broadcasted_iota(jnp.int32, (tm, 1), 0).astype(F32)

    for hh in range(RET_HEADS):
        lg = lgs[hh]
        q = _rot(proj_ref[:, hh * RET_DK:(hh + 1) * RET_DK], cos, sin) * (RET_DK ** -0.5)
        k = _rot(proj_ref[:, D_MODEL + hh * RET_DK:D_MODEL + (hh + 1) * RET_DK], cos, sin)
        v = proj_ref[:, 2 * D_MODEL + hh * RET_DV:2 * D_MODEL + (hh + 1) * RET_DV].astype(BF16)
        qb = q.astype(BF16)
        dmask = jnp.where(causal, jnp.exp(diff * lg), 0.0)
        scores = lax.dot_general(qb, k.astype(BF16), (((1,), (1,)), ((), ())), preferred_element_type=F32) * dmask
        s_old = so_ref[0, hh]
        o = jnp.dot(scores.astype(BF16), v, preferred_element_type=F32)
        o = o + jnp.dot(qb, s_old.astype(BF16), preferred_element_type=F32) * jnp.exp((idx + 1.0) * lg)
        k_dec = (k * jnp.exp((tm - 1.0 - idx) * lg)).astype(BF16)
        so_ref[0, hh] = s_old * jnp.exp(jnp.full((1, 1), tm * lg, F32)) + lax.dot_general(
            k_dec, v, (((0,), (0,)), ((), ())), preferred_element_type=F32)
        o = o * lax.rsqrt(jnp.mean(o * o, axis=-1, keepdims=True) + NORM_EPS)
        o = o * gng_ref[:, hh * RET_DV:(hh + 1) * RET_DV]
        gate = proj_ref[:, 2 * D_MODEL + D_RET_V + hh * RET_DV:2 * D_MODEL + D_RET_V + (hh + 1) * RET_DV]
        y_ref[:, hh * RET_DV:(hh + 1) * RET_DV] = (o * _silu(gate)).astype(BF16)

    xn = x + jnp.dot(y_ref[...], wout_ref[...], preferred_element_type=F32)
    xo_ref[0] = _rms(xn, fg_ref[...])


def _layer_d(x, s0, pos0, g, w_in, gn_g, w_out, final_g, *, tm):
    B, T, D = x.shape
    lgs =tuple(math.log1p(-(2.0 ** (-5.0 - hh))) for hh in range(RET_HEADS))
    half = RET_DK // 2
    inv = ROPE_BASE ** (-jnp.arange(half, dtype=F32) / half)
    ang = (pos0 + jnp.arange(T)).astype(F32)[:, None] * inv[None]
    cos, sin = jnp.cos(ang), jnp.sin(ang)
    row = lambda v: v.reshape(1, -1)
    nproj = 2 * D + 2 * D_RET_V
    xo, so = pl.pallas_call(
        functools.partial(_layer_d_kernel, tm=tm, lgs=lgs),
        grid=(B, T // tm),
        in_specs=[
            pl.BlockSpec((1, tm, D), lambda b, t: (b, t, 0)),
            pl.BlockSpec((1, RET_HEADS, RET_DK, RET_DV), lambda b, t: (b, 0, 0, 0)),
            pl.BlockSpec((tm, half), lambda b, t: (t, 0)),
            pl.BlockSpec((tm, half), lambda b, t: (t, 0)),
            _const_spec((1, D)),
            _const_spec((D, nproj)),
            _const_spec((1, D_RET_V)),
            _const_spec((D_RET_V, D)),
            _const_spec((1, D)),
        ],
        out_specs=[
            pl.BlockSpec((1, tm, D), lambda b, t: (b, t, 0)),
            pl.BlockSpec((1, RET_HEADS, RET_DK, RET_DV), lambda b, t: (b, 0, 0, 0)),
        ],
        out_shape=[jax.ShapeDtypeStruct((B, T, D), F32),
                   jax.ShapeDtypeStruct((B, RET_HEADS, RET_DK, RET_DV), F32)],
        scratch_shapes=[
            pltpu.VMEM((tm, nproj), F32),
            pltpu.VMEM((tm, D_RET_V), BF16),
        ],
        compiler_params=pltpu.CompilerParams(
            dimension_semantics=("arbitrary", "arbitrary"), vmem_limit_bytes=VMEM_LIMIT),
        name="layer_d",
    )(x, s0, cos, sin, row(g), w_in, row(gn_g), w_out, row(final_g))
    return xo, so


RWKV_HEAD = 64
RWKV_HEADS = 16
D_LORA = 64
RWKV_GN_EPS = 64e-5


def _seg_mats():
    c = jnp.arange(D_MODEL)[:, None] // RWKV_HEAD
    hcol = jnp.arange(LANES)[None, :]
    seg = (c == hcol).astype(BF16)
    return seg, seg.T


def _segsum(z, seg):
    return jnp.dot(z.astype(BF16), seg, preferred_element_type=F32)


def _segbcast(s, bc):
    hi = s.astype(BF16)
    lo = (s - hi.astype(F32)).astype(BF16)
    return jnp.dot(hi, bc, preferred_element_type=F32) + jnp.dot(lo, bc, preferred_element_type=F32)


def _layer_c1_kernel(x_ref, sh0_ref, g_ref, mu_ref, win_ref, w0_ref, w1_ref, w2_ref, a0_ref, a1_ref, a2_ref,
                     ka_ref, rk_ref, seg_ref, bc_ref,
                     r_ref, w_ref, k_ref, v_ref, a_ref, gt_ref, bonus_ref, sho_ref, carry_ref, *, tm):
    ti = pl.program_id(1)

    @pl.when(ti == 0)
    def _():
        carry_ref[...] = sh0_ref[0]

    h = _rms(x_ref[0], g_ref[...])
    rows = lax.broadcasted_iota(jnp.int32, h.shape, 0)
    h_prev = jnp.where(rows == 0, carry_ref[...], pltpu.roll(h, 1, 0))
    carry_ref[...] = h[tm - 1:tm, :]
    sho_ref[0] = h[tm - 1:tm, :]
    xx = h_prev - h

    def mix(s):
        return (h + xx * mu_ref[s:s + 1, :]).astype(BF16)

    r = jnp.dot(mix(0), win_ref[:, 0:D_MODEL], preferred_element_type=F32)
    k = jnp.dot(mix(1), win_ref[:, D_MODEL:2 * D_MODEL], preferred_element_type=F32)
    v = jnp.dot(mix(2), win_ref[:, 2 * D_MODEL:3 * D_MODEL], preferred_element_type=F32)
    gt_ref[0] = jnp.dot(mix(3), win_ref[:, 3 * D_MODEL:4 * D_MODEL], preferred_element_type=F32)
    r_ref[0] = r
    v_ref[0] = v

    wl = jnp.tanh(jnp.dot(mix(4), w1_ref[...], preferred_element_type=F32))
    wx = w0_ref[...] + _dot(wl, w2_ref[...])
    w_log = -(jnp.maximum(-wx, 0.0) + jnp.log1p(jnp.exp(-jnp.abs(wx)))) - 0.5
    w_ref[0] = jnp.exp(-jnp.exp(w_log))

    al = jnp.dot(mix(5), a1_ref[...], preferred_element_type=F32)
    a = jax.nn.sigmoid(a0_ref[...] + _dot(al, a2_ref[...]))

    k_ref[0] = k
    a_ref[0] = a
    kh = k * (1.0 + (a - 1.0) * ka_ref[...])
    rk = _segsum(r * kh * rk_ref[...], seg_ref[...])
    bonus_ref[0] = _segbcast(rk, bc_ref[...]) * v


def _layer_c1(x, sh0, g, mu, w_in, w0, w1, w2, a0, a1, a2, k_a, r_k, *, tm):
    B, T, D = x.shape
    seg, bc = _seg_mats()
    row = lambda v: v.reshape(1, -1)
    tile = pl.BlockSpec((1, tm, D), lambda b, t: (b, t, 0))
    outs = pl.pallas_call(
        functools.partial(_layer_c1_kernel, tm=tm),
        grid=(B, T // tm),
        in_specs=[
            tile,
            pl.BlockSpec((1, 1, D), lambda b, t: (b, 0, 0)),
            _const_spec((1, D)),
            _const_spec((6, D)),
            _const_spec((D, 4 * D)),
            _const_spec((1, D)),
            _const_spec((D, D_LORA)),
            _const_spec((D_LORA, D)),
            _const_spec((1, D)),
            _const_spec((D, D_LORA)),
            _const_spec((D_LORA, D)),
            _const_spec((1, D)),
            _const_spec((1, D)),
            _const_spec((D, LANES)),
            _const_spec((LANES, D)),
        ],
        out_specs=[tile] * 7 + [pl.BlockSpec((1, 1, D), lambda b, t: (b, 0, 0))],
        out_shape=[jax.ShapeDtypeStruct((B, T, D), F32)] * 7 + [jax.ShapeDtypeStruct((B, 1, D), F32)],
        scratch_shapes=[pltpu.VMEM((1, D), F32)],
        compiler_params=pltpu.CompilerParams(
            dimension_semantics=("arbitrary", "arbitrary"), vmem_limit_bytes=VMEM_LIMIT),
        name="layer_c1",
    )(x, sh0.reshape(B, 1, D), row(g), mu, w_in, row(w0), w1, w2, row(a0), a1, a2, row(k_a), row(r_k), seg, bc)
    return outs[:7], outs[7].reshape(B, D)


def _layer_c3_kernel(x_ref, o_ref, bonus_ref, gt_ref, gng_ref, gnb_ref, wout_ref, xo_ref):
    o = o_ref[0] * gng_ref[...] + gnb_ref[...]
    y = (o + bonus_ref[0]) * _silu(gt_ref[0])
    xo_ref[0] = x_ref[0] + _dot(y, wout_ref[...])


def _layer_c3(x, o, bonus, gt, gn_g, gn_b, w_out, *, tm):
    B, T, D = x.shape
    row = lambda v: v.reshape(1, -1)
    tile = pl.BlockSpec((1, tm, D), lambda b, t: (b, t, 0))
    return pl.pallas_call(
        _layer_c3_kernel,
        grid=(B, T // tm),
        in_specs=[tile, tile, tile, tile, _const_spec((1, D)), _const_spec((1, D)), _const_spec((D, D))],
        out_specs=tile,
        out_shape=jax.ShapeDtypeStruct((B, T, D), F32),
        compiler_params=pltpu.CompilerParams(
            dimension_semantics=("arbitrary", "arbitrary"), vmem_limit_bytes=VMEM_LIMIT),
        name="layer_c3",
    )(x, o, bonus, gt, row(gn_g), row(gn_b), w_out)


def _wkv_kernel(r_ref, w_ref, k_ref, a_ref, v_ref, kkt_ref, kat_ref, s0_ref, y_ref, so_ref,
                col_ref, vt_ref, yt_ref, *, tq, nb, packed):
    @pl.when(pl.program_id(0) == 0)
    def _():
        so_ref[...] = s0_ref[...]

    half = LANES // 2
    npar = 2 if packed else 1
    nrow = RWKV_HEAD // npar
    low = lax.broadcasted_iota(jnp.int32, (RWKV_HEAD, LANES), 1) < half
    low_r = lax.broadcasted_iota(jnp.int32, (nrow, LANES), 1) < half
    Q_R, Q_W, Q_K, Q_A, Q_B, Q_V = range(6)
    nblk = tq // SUBLANES

    def relayout(blk):
        t8 = pl.multiple_of(blk * SUBLANES, SUBLANES)
        for q, ref in ((Q_R, r_ref), (Q_W, w_ref), (Q_K, k_ref), (Q_A, a_ref), (Q_V, v_ref)):
            slabs = [ref[b, pl.ds(t8, SUBLANES), :].reshape(SUBLANES, RWKV_HEADS, RWKV_HEAD) for b in range(nb)]
            for g in range(SUBLANES // npar):
                mt = jnp.concatenate([slabs[b][npar * g + par] for par in range(npar) for b in range(nb)], axis=0).T
                t0 = t8 + npar * g
                if not packed:
                    if q == Q_V:
                        vt_ref[t0] = mt
                    else:
                        col_ref[t0, q] = mt
                elif q == Q_V:
                    top, bot = mt[0:nrow], mt[nrow:RWKV_HEAD]
                    vt_ref[t0] = jnp.where(low_r, top, pltpu.roll(bot, half, 1))
                    vt_ref[t0 + 1] = jnp.where(low_r, pltpu.roll(top, half, 1), bot)
                else:
                    rolled = pltpu.roll(mt, half, 1)
                    col_ref[t0, q] = jnp.where(low, mt, rolled)
                    col_ref[t0 + 1, q] = jnp.where(low, rolled, mt)
        for i in range(SUBLANES):
            t = t8 + i
            k, a = col_ref[t, Q_K], col_ref[t, Q_A]
            kk = k * kkt_ref[...]
            n2 = jnp.sum(kk * kk, axis=0, keepdims=True)
            kk = kk * jnp.minimum(lax.rsqrt(n2), 1e12)
            col_ref[t, Q_K] = k * (1.0 + (a - 1.0) * kat_ref[...])
            col_ref[t, Q_A] = -kk
            col_ref[t, Q_B] = kk * a

    def head_norm(o):
        d = o - jnp.mean(o, axis=0, keepdims=True)
        var = jnp.mean(d * d, axis=0, keepdims=True)
        return d * lax.rsqrt(var + RWKV_GN_EPS)

    def writeback(blk):
        t8 = pl.multiple_of(blk * SUBLANES, SUBLANES)
        slabs = [[None] * SUBLANES for _ in range(nb)]
        for g in range(SUBLANES // npar):
            t0 = t8 + npar * g
            if packed:
                y0, y1 = yt_ref[t0], yt_ref[t0 + 1]
                yc = jnp.concatenate([jnp.where(low_r, y0, pltpu.roll(y1, half, 1)),
                                      jnp.where(low_r, pltpu.roll(y0, half, 1), y1)], axis=0)
            else:
                yc = yt_ref[t0]
            yt = head_norm(yc).T
            for par in range(npar):
                for b in range(nb):
                    r0 = (par * nb + b) * RWKV_HEADS
                    slabs[b][npar * g + par] = yt[r0:r0 + RWKV_HEADS, :]
        for b in range(nb):
            y_ref[b, pl.ds(t8, SUBLANES), :] = jnp.stack(slabs[b], axis=0).reshape(SUBLANES, D_MODEL)

    def token(t, sa):
        tn = jnp.minimum(t + 1, tq - 1)
        vt = vt_ref[t]
        y, nxt = [None, None], [None, None]
        for j in range(RWKV_HEAD):
            row = lambda q: col_ref[t, q, j:j + 1, :]
            s = so_ref[j] * row(Q_W) + sa * row(Q_B) + vt * row(Q_K)
            so_ref[j] = s
            yj, nj = s * row(Q_R), s * col_ref[tn, Q_A, j:j + 1, :]
            y[j % 2] = yj if y[j % 2] is None else y[j % 2] + yj
            nxt[j % 2] = nj if nxt[j % 2] is None else nxt[j % 2] + nj
        yt_ref[t] = y[0] + y[1]
        return nxt[0] + nxt[1]

    def relayout_blocks(u, carry):
        for n in range(2):
            relayout(2 * u + n)
        return carry

    def writeback_blocks(u, carry):
        for n in range(2):
            writeback(2 * u + n)
        return carry

    lax.fori_loop(0, nblk // 2, relayout_blocks, 0)

    acc = [so_ref[0] * col_ref[0, Q_A, 0:1, :], so_ref[1] * col_ref[0, Q_A, 1:2, :]]
    for j in range(2, RWKV_HEAD):
        acc[j % 2] = acc[j % 2] + so_ref[j] * col_ref[0, Q_A, j:j + 1, :]
    lax.fori_loop(0, tq, token, acc[0] + acc[1])

    lax.fori_loop(0, nblk // 2, writeback_blocks, 0)


def _wkv_layer(r, w, k, v, a, k_k, k_a, s0):
    B, T, D = r.shape
    H, N = RWKV_HEADS, RWKV_HEAD
    packed = B * H == LANES // 2
    assert packed or B * H == LANES
    if packed:
        ni = N // 2
        st_in = s0.reshape(B, H, 2, ni, N).transpose(4, 3, 2, 0, 1).reshape(N, ni, LANES)
        st_out = lambda s: s.reshape(N, ni, 2, B, H).transpose(3, 4, 2, 1, 0).reshape(B, H, N, N)
    else:
        ni = N
        st_in = s0.transpose(3, 2, 0, 1).reshape(N, N, LANES)
        st_out = lambda s: s.reshape(N, N, B, H).transpose(2, 3, 1, 0)
    tq = min(T, 64)
    tok_spec = pl.BlockSpec((B, tq, D), lambda t: (0, t, 0))
    st_spec = pl.BlockSpec((N, ni, LANES), lambda t: (0, 0, 0))
    par_spec = pl.BlockSpec((N, LANES), lambda t: (0, 0))
    lane_tile = lambda p: jnp.tile(p.reshape(H, N).T, (1, LANES // H))
    y, s = pl.pallas_call(
        functools.partial(_wkv_kernel, tq=tq, nb=B, packed=packed),
        grid=(T // tq,),
        in_specs=[tok_spec] * 5 + [par_spec, par_spec, st_spec],
        out_specs=[tok_spec, st_spec],
        out_shape=[jax.ShapeDtypeStruct((B, T, D), F32), jax.ShapeDtypeStruct((N, ni, LANES), F32)],
        scratch_shapes=[pltpu.VMEM((tq, 5, N, LANES), F32), pltpu.VMEM((tq, ni, LANES), F32),
                        pltpu.VMEM((tq, ni, LANES), F32)],
        compiler_params=pltpu.CompilerParams(dimension_semantics=("arbitrary",), vmem_limit_bytes=VMEM_LIMIT),
        name="wkv",
    )(r, w, k, a, v, lane_tile(k_k), lane_tile(k_a), st_in)
    return y, st_out(s)


def kernel(x_prompt, x_sample, cache_conv_a, cache_conv_b, state_lru_b, state_shift_c, state_wkv_c, state_ret_d, meta_tokens, norm_g, final_norm_g, a_w_in, a_w_dw, a_b_dw, a_ln_g, a_ln_b, a_w_out, b_w_in, b_w_conv, b_b_conv, b_w_rg, b_b_rg, b_w_ig, b_b_ig, b_lam, b_w_out, c_mu, c_w_in, c_w0, c_w1, c_w2, c_a0, c_a1, c_a2, c_k_k, c_k_a, c_r_k, c_gn_g, c_gn_b, c_w_out, d_w_in, d_gn_g, d_w_out):
    bf = lambda z: z.astype(BF16)
    a_win, a_wout = bf(a_w_in), bf(a_w_out)
    b_win, b_wout = bf(b_w_in), bf(b_w_out)
    b_wg = bf(jnp.concatenate([b_w_rg, b_w_ig], axis=-1))
    c_win, c_wout = bf(c_w_in), bf(c_w_out)
    c_w1b, c_w2b, c_a1b, c_a2b = bf(c_w1), bf(c_w2), bf(c_a1), bf(c_a2)
    d_win, d_wout = bf(d_w_in), bf(d_w_out)

    def trunk(x, st, pos0, tm):
        x, conv_a = _layer_a(x, st[0], norm_g[0], a_win, a_w_dw, a_b_dw, a_ln_g, a_ln_b, a_wout, tm=tm)
        x, conv_b, lru = _layer_b(x, st[1], st[2], norm_g[1], b_win, b_w_conv, b_b_conv, b_wg, b_b_rg, b_b_ig,
                                  b_lam, b_wout, tm=tm)
        (r, w, k, v, a, gt, bonus), shift = _layer_c1(
            x, st[3], norm_g[2], c_mu, c_win, c_w0, c_w1b, c_w2b, c_a0, c_a1b, c_a2b, c_k_a,
            c_r_k.reshape(-1), tm=tm)
        o, wkv = _wkv_layer(r, w, k, v, a, c_k_k, c_k_a, st[4])
        x = _layer_c3(x, o, bonus, gt, c_gn_g, c_gn_b, c_wout, tm=tm)
        x, ret = _layer_d(x, st[5], pos0, norm_g[3], d_win, d_gn_g, d_wout, final_norm_g, tm=tm)
        return x, (conv_a, conv_b, lru, shift, wkv, ret)

    B = x_prompt.shape[0]
    zeros = (jnp.zeros((B, CONV_W - 1, D_MODEL), F32), jnp.zeros((B, LRU_CONV_W - 1, D_RNN), F32),
             jnp.zeros((B, D_RNN), F32), jnp.zeros((B, D_MODEL), F32),
             jnp.zeros((B, RWKV_HEADS, RWKV_HEAD, RWKV_HEAD), F32), jnp.zeros((B, RET_HEADS, RET_DK, RET_DV), F32))
    meta = jnp.broadcast_to(meta_tokens[None], (B, N_META, D_MODEL))
    _, st_meta = trunk(meta, zeros, 0, N_META)
    y_prompt, stp = trunk(x_prompt, st_meta, N_META, 256)
    st_s = (cache_conv_a, cache_conv_b, state_lru_b, state_shift_c, state_wkv_c, state_ret_d)
    y_sample, sts = trunk(x_sample, st_s, N_META + PAST_LEN, x_sample.shape[1])
    return (y_prompt, y_sample, stp[0], sts[0], stp[1], sts[1], stp[2], sts[2], stp[3], sts[3],
            stp[4], sts[4], stp[5], sts[5])
```

```python
import functools
import math

import jax
import jax.numpy as jnp
from jax import lax
from jax.experimental import pallas as pl
from jax.experimental.pallas import tpu as pltpu

F32 = jnp.float32
BF16 = jnp.bfloat16

D_MODEL = 1024
N_META = 16
PAST_LEN = 1024
NORM_EPS = 1e-6
LN_EPS = 1e-5
CONV_W = 31
HIST = 32
LANES = 128
SUBLANES = 8

VMEM_LIMIT = 56 * 1024 * 1024


def _rms(x, g):
    return x * lax.rsqrt(jnp.mean(x * x, axis=-1, keepdims=True) + NORM_EPS) * g


def _silu(x):
    return x * jax.nn.sigmoid(x)


def _dot(a, b):
    return jnp.dot(a.astype(BF16), b, preferred_element_type=F32)


def _dwconv_block(hist_ref, w_ref, b_ref, r0, rc, ls, width, nhist):
    off = nhist - (width - 1)
    nwin = rc + nhist
    win = hist_ref[r0:r0 + nwin, ls]
    acc = jnp.zeros((rc, ls.stop - ls.start), F32) + b_ref[:, ls]
    for res in range(SUBLANES):
        taps = [s for s in range(res, nhist + 1, SUBLANES) if 0 <= s - off < width]
        if not taps:
            continue
        wr = win if res == 0 else pltpu.roll(win, nwin - res, 0)
        for s in taps:
            acc = acc + w_ref[s - off:s - off + 1, ls] * wr[s - res:s - res + rc]
    return acc


def _const_spec(shape):
    n = len(shape)
    return pl.BlockSpec(shape, lambda b, t: (0,) * n, pipeline_mode=pl.Buffered(1))


def _layer_a_kernel(x_ref, buf0_ref, g_ref, win_ref, wdw_ref, bdw_ref, lng_ref, lnb_ref, wout_ref,
                    xo_ref, bufo_ref, proj_ref, hist_ref, c_ref, y_ref, *, tm):
    ti = pl.program_id(1)

    @pl.when(ti == 0)
    def _():
        hist_ref[0:HIST, :] = buf0_ref[0]

    x = x_ref[0]
    h = _rms(x, g_ref[...])
    proj_ref[...] = _dot(h, win_ref[...])
    hist_ref[HIST:HIST + tm, :] = proj_ref[:, 0:D_MODEL] * jax.nn.sigmoid(proj_ref[:, D_MODEL:2 * D_MODEL])

    rc = min(tm, 64)
    for r in range(tm // rc):
        for lb in range(D_MODEL // LANES):
            ls = slice(lb * LANES, (lb + 1) * LANES)
            c_ref[r * rc:(r + 1) * rc, ls] = _dwconv_block(hist_ref, wdw_ref, bdw_ref, r * rc, rc, ls, CONV_W, HIST)

    rc = min(tm, 16)
    for r in range(tm // rc):
        acc = c_ref[r * rc:(r + 1) * rc, :]
        mu = jnp.mean(acc, axis=-1, keepdims=True)
        d = acc - mu
        var = jnp.mean(d * d, axis=-1, keepdims=True)
        c = d * lax.rsqrt(var + LN_EPS) * lng_ref[...] + lnb_ref[...]
        gate = proj_ref[r * rc:(r + 1) * rc, 2 * D_MODEL:3 * D_MODEL]
        y_ref[r * rc:(r + 1) * rc, :] = (_silu(c) * _silu(gate)).astype(BF16)

    xo_ref[0] = x + jnp.dot(y_ref[...], wout_ref[...], preferred_element_type=F32)

    if tm >= HIST:
        hist_ref[0:HIST, :] = hist_ref[tm:tm + HIST, :]
    else:
        hist_ref[0:HIST - tm, :] = hist_ref[tm:HIST, :]
        hist_ref[HIST - tm:HIST, :] = hist_ref[HIST:HIST + tm, :]
    bufo_ref[0] = hist_ref[0:HIST, :]


def _layer_a(x, buf, g, w_in, w_dw, b_dw, ln_g, ln_b, w_out, *, tm):
    B, T, D = x.shape
    buf32 = jnp.concatenate([jnp.zeros((B, HIST - (CONV_W - 1), D), F32), buf], axis=1)
    row = lambda v: v.reshape(1, -1)
    xo, bufo = pl.pallas_call(
        functools.partial(_layer_a_kernel, tm=tm),
        grid=(B, T // tm),
        in_specs=[
            pl.BlockSpec((1, tm, D), lambda b, t: (b, t, 0)),
            pl.BlockSpec((1, HIST, D), lambda b, t: (b, 0, 0)),
            _const_spec((1, D)),
            _const_spec((D, 3 * D)),
            _const_spec((CONV_W, D)),
            _const_spec((1, D)),
            _const_spec((1, D)),
            _const_spec((1, D)),
            _const_spec((D, D)),
        ],
        out_specs=[
            pl.BlockSpec((1, tm, D), lambda b, t: (b, t, 0)),
            pl.BlockSpec((1, HIST, D), lambda b, t: (b, 0, 0)),
        ],
        out_shape=[jax.ShapeDtypeStruct((B, T, D), F32), jax.ShapeDtypeStruct((B, HIST, D), F32)],
        scratch_shapes=[
            pltpu.VMEM((tm, 3 * D), F32),
            pltpu.VMEM((HIST + tm, D), F32),
            pltpu.VMEM((tm, D), F32),
            pltpu.VMEM((tm, D), BF16),
        ],
        compiler_params=pltpu.CompilerParams(
            dimension_semantics=("arbitrary", "arbitrary"), vmem_limit_bytes=VMEM_LIMIT),
        name="layer_a",
    )(x, buf32, row(g), w_in, w_dw, row(b_dw), row(ln_g), row(ln_b), w_out)
    return xo, bufo[:, HIST - (CONV_W - 1):]


D_RNN = 1280
LRU_BLOCKS = 10
LRU_BLOCK = 128
LRU_CONV_W = 4
LRU_C = 8.0
BHIST = 8


def _scan_affine_rows(a, b):
    n = a.shape[0]
    rows = lax.broadcasted_iota(jnp.int32, a.shape, 0)
    k = 1
    while k < n:
        a_s = jnp.where(rows >= k, pltpu.roll(a, k, 0), 1.0)
        b_s = jnp.where(rows >= k, pltpu.roll(b, k, 0), 0.0)
        b = a * b_s + b
        a = a * a_s
        k *= 2
    return a, b


def _one_minus_exp(z, ez):
    s = 1.0 / 120.0
    for c in (1.0 / 24.0, 1.0 / 6.0, 0.5, 1.0):
        s = c + z * s
    return jnp.where(z > -1.0 / 32.0, -z * s, 1.0 - ez)


def _sqrt_nonneg(x):
    return jnp.where(x > 0.0, x * lax.rsqrt(x), 0.0)


def _layer_b_kernel(x_ref, buf0_ref, h0_ref, g_ref, win_ref, wc_ref, bc_ref, wg_ref, brg_ref, big_ref, lam_ref,
                    wout_ref, xo_ref, bufo_ref, ho_ref, proj_ref, hist_ref, carry_ref, y_ref, *, tm):
    ti = pl.program_id(1)

    @pl.when(ti == 0)
    def _():
        hist_ref[0:BHIST, :] = buf0_ref[0]
        carry_ref[...] = h0_ref[0]

    x = x_ref[0]
    h = _rms(x, g_ref[...])
    proj_ref[...] = _dot(h, win_ref[...])
    hist_ref[BHIST:BHIST + tm, :] = proj_ref[:, 0:D_RNN]

    lam = lam_ref[...]
    neg_c_sp = -LRU_C * (jnp.maximum(-lam, 0.0) + jnp.log1p(jnp.exp(-jnp.abs(lam))))

    rc = min(tm, 64)
    for r in range(tm // rc):
        r0 = r * rc
        xc = jnp.concatenate(
            [_dwconv_block(hist_ref, wc_ref, bc_ref, r0, rc, slice(lb * LANES, (lb + 1) * LANES), LRU_CONV_W, BHIST)
             for lb in range(D_RNN // LANES)], axis=1)
        xcb = xc.astype(BF16)
        rg, ig = [], []
        for n in range(LRU_BLOCKS):
            gg = jnp.dot(xcb[:, n * LRU_BLOCK:(n + 1) * LRU_BLOCK], wg_ref[n], preferred_element_type=F32)
            rg.append(gg[:, 0:LRU_BLOCK])
            ig.append(gg[:, LRU_BLOCK:2 * LRU_BLOCK])
        rgate = jax.nn.sigmoid(jnp.concatenate(rg, axis=1) + brg_ref[...])
        igate = jax.nn.sigmoid(jnp.concatenate(ig, axis=1) + big_ref[...])
        log_a = neg_c_sp * rgate
        a = jnp.exp(log_a)
        bx = _sqrt_nonneg(_one_minus_exp(2.0 * log_a, a * a)) * (igate * xc)
        a_cum, b_cum = _scan_affine_rows(a, bx)
        hs = a_cum * carry_ref[...] + b_cum
        carry_ref[...] = hs[rc - 1:rc, :]
        gate = proj_ref[r0:r0 + rc, D_RNN:2 * D_RNN]
        y_ref[r0:r0 + rc, :] = (hs * _silu(gate)).astype(BF16)

    xo_ref[0] = x + jnp.dot(y_ref[...], wout_ref[...], preferred_element_type=F32)
    hist_ref[0:BHIST, :] = hist_ref[tm:tm + BHIST, :]
    bufo_ref[0] = hist_ref[0:BHIST, :]
    ho_ref[0] = carry_ref[...]


def _layer_b(x, buf, h0, g, w_in, w_conv, b_conv, w_g, b_rg, b_ig, lam, w_out, *, tm):
    B, T, D = x.shape
    buf8 = jnp.concatenate([jnp.zeros((B, BHIST - (LRU_CONV_W - 1), D_RNN), F32), buf], axis=1)
    row = lambda v: v.reshape(1, -1)
    xo, bufo, ho = pl.pallas_call(
        functools.partial(_layer_b_kernel, tm=tm),
        grid=(B, T // tm),
        in_specs=[
            pl.BlockSpec((1, tm, D), lambda b, t: (b, t, 0)),
            pl.BlockSpec((1, BHIST, D_RNN), lambda b, t: (b, 0, 0)),
            pl.BlockSpec((1, 1, D_RNN), lambda b, t: (b, 0, 0)),
            _const_spec((1, D)),
            _const_spec((D, 2 * D_RNN)),
            _const_spec((LRU_CONV_W, D_RNN)),
            _const_spec((1, D_RNN)),
            _const_spec((LRU_BLOCKS, LRU_BLOCK, 2 * LRU_BLOCK)),
            _const_spec((1, D_RNN)),
            _const_spec((1, D_RNN)),
            _const_spec((1, D_RNN)),
            _const_spec((D_RNN, D)),
        ],
        out_specs=[
            pl.BlockSpec((1, tm, D), lambda b, t: (b, t, 0)),
            pl.BlockSpec((1, BHIST, D_RNN), lambda b, t: (b, 0, 0)),
            pl.BlockSpec((1, 1, D_RNN), lambda b, t: (b, 0, 0)),
        ],
        out_shape=[jax.ShapeDtypeStruct((B, T, D), F32), jax.ShapeDtypeStruct((B, BHIST, D_RNN), F32),
                   jax.ShapeDtypeStruct((B, 1, D_RNN), F32)],
        scratch_shapes=[
            pltpu.VMEM((tm, 2 * D_RNN), F32),
            pltpu.VMEM((BHIST + tm, D_RNN), F32),
            pltpu.VMEM((1, D_RNN), F32),
            pltpu.VMEM((tm, D_RNN), BF16),
        ],
        compiler_params=pltpu.CompilerParams(
            dimension_semantics=("arbitrary", "arbitrary"), vmem_limit_bytes=VMEM_LIMIT),
        name="layer_b",
    )(x, buf8, h0.reshape(B, 1, D_RNN), row(g), w_in, w_conv, row(b_conv), w_g, row(b_rg), row(b_ig), row(lam), w_out)
    return xo, bufo[:, BHIST - (LRU_CONV_W - 1):], ho.reshape(B, D_RNN)


RET_HEADS = 4
RET_DK = 256
RET_DV = 512
D_RET_V = RET_HEADS * RET_DV
ROPE_BASE = 10000.0


def _rot(z, cos, sin):
    half = RET_DK // 2
    z1, z2 = z[:, 0:half], z[:, half:RET_DK]
    return jnp.concatenate([z1 * cos - z2 * sin, z1 * sin + z2 * cos], axis=1)


def _layer_d_kernel(x_ref, s0_ref, cos_ref, sin_ref, g_ref, win_ref, gng_ref, wout_ref, fg_ref,
                    xo_ref, so_ref, proj_ref, y_ref, *, tm, lgs):
    ti = pl.program_id(1)

    @pl.when(ti == 0)
    def _():
        so_ref[...] = s0_ref[...]

    x = x_ref[0]
    h = _rms(x, g_ref[...])
    proj_ref[...] = _dot(h, win_ref[...])
    cos = cos_ref[...]
    sin = sin_ref[...]
    row = lax.broadcasted_iota(jnp.int32, (tm, tm), 0)
    col = lax.broadcasted_iota(jnp.int32, (tm, tm), 1)
    causal = row >= col
    diff = jnp.where(causal, row - col, 0).astype(F32)
    idx = lax.broadcasted_iota(jnp.int32, (tm, 1), 0).astype(F32)

    for hh in range(RET_HEADS):
        lg = lgs[hh]
        q = _rot(proj_ref[:, hh * RET_DK:(hh + 1) * RET_DK], cos, sin) * (RET_DK ** -0.5)
        k = _rot(proj_ref[:, D_MODEL + hh * RET_DK:D_MODEL + (hh + 1) * RET_DK], cos, sin)
        v = proj_ref[:, 2 * D_MODEL + hh * RET_DV:2 * D_MODEL + (hh + 1) * RET_DV].astype(BF16)
        qb = q.astype(BF16)
        dmask = jnp.where(causal, jnp.exp(diff * lg), 0.0)
        scores = lax.dot_general(qb, k.astype(BF16), (((1,), (1,)), ((), ())), preferred_element_type=F32) * dmask
        s_old = so_ref[0, hh]
        o = jnp.dot(scores.astype(BF16), v, preferred_element_type=F32)
        o = o + jnp.dot(qb, s_old.astype(BF16), preferred_element_type=F32) * jnp.exp((idx + 1.0) * lg)
        k_dec = (k * jnp.exp((tm - 1.0 - idx) * lg)).astype(BF16)
        so_ref[0, hh] = s_old * jnp.exp(jnp.full((1, 1), tm * lg, F32)) + lax.dot_general(
            k_dec, v, (((0,), (0,)), ((), ())), preferred_element_type=F32)
        o = o * lax.rsqrt(jnp.mean(o * o, axis=-1, keepdims=True) + NORM_EPS)
        o = o * gng_ref[:, hh * RET_DV:(hh + 1) * RET_DV]
        gate = proj_ref[:, 2 * D_MODEL + D_RET_V + hh * RET_DV:2 * D_MODEL + D_RET_V + (hh + 1) * RET_DV]
        y_ref[:, hh * RET_DV:(hh + 1) * RET_DV] = (o * _silu(gate)).astype(BF16)

    xn = x + jnp.dot(y_ref[...], wout_ref[...], preferred_element_type=F32)
    xo_ref[0] = _rms(xn, fg_ref[...])


def _layer_d(x, s0, pos0, g, w_in, gn_g, w_out, final_g, *, tm):
    B, T, D = x.shape
    lgs =tuple(math.log1p(-(2.0 ** (-5.0 - hh))) for hh in range(RET_HEADS))
    half = RET_DK // 2
    inv = ROPE_BASE ** (-jnp.arange(half, dtype=F32) / half)
    ang = (pos0 + jnp.arange(T)).astype(F32)[:, None] * inv[None]
    cos, sin = jnp.cos(ang), jnp.sin(ang)
    row = lambda v: v.reshape(1, -1)
    nproj = 2 * D + 2 * D_RET_V
    xo, so = pl.pallas_call(
        functools.partial(_layer_d_kernel, tm=tm, lgs=lgs),
        grid=(B, T // tm),
        in_specs=[
            pl.BlockSpec((1, tm, D), lambda b, t: (b, t, 0)),
            pl.BlockSpec((1, RET_HEADS, RET_DK, RET_DV), lambda b, t: (b, 0, 0, 0)),
            pl.BlockSpec((tm, half), lambda b, t: (t, 0)),
            pl.BlockSpec((tm, half), lambda b, t: (t, 0)),
            _const_spec((1, D)),
            _const_spec((D, nproj)),
            _const_spec((1, D_RET_V)),
            _const_spec((D_RET_V, D)),
            _const_spec((1, D)),
        ],
        out_specs=[
            pl.BlockSpec((1, tm, D), lambda b, t: (b, t, 0)),
            pl.BlockSpec((1, RET_HEADS, RET_DK, RET_DV), lambda b, t: (b, 0, 0, 0)),
        ],
        out_shape=[jax.ShapeDtypeStruct((B, T, D), F32),
                   jax.ShapeDtypeStruct((B, RET_HEADS, RET_DK, RET_DV), F32)],
        scratch_shapes=[
            pltpu.VMEM((tm, nproj), F32),
            pltpu.VMEM((tm, D_RET_V), BF16),
        ],
        compiler_params=pltpu.CompilerParams(
            dimension_semantics=("arbitrary", "arbitrary"), vmem_limit_bytes=VMEM_LIMIT),
        name="layer_d",
    )(x, s0, cos, sin, row(g), w_in, row(gn_g), w_out, row(final_g))
    return xo, so


RWKV_HEAD = 64
RWKV_HEADS = 16
D_LORA = 64
RWKV_GN_EPS = 64e-5


def _seg_mats():
    c = jnp.arange(D_MODEL)[:, None] // RWKV_HEAD
    hcol = jnp.arange(LANES)[None, :]
    seg = (c == hcol).astype(BF16)
    return seg, seg.T


def _segsum(z, seg):
    return jnp.dot(z.astype(BF16), seg, preferred_element_type=F32)


def _segbcast(s, bc):
    hi = s.astype(BF16)
    lo = (s - hi.astype(F32)).astype(BF16)
    return jnp.dot(hi, bc, preferred_element_type=F32) + jnp.dot(lo, bc, preferred_element_type=F32)


def _layer_c1_kernel(x_ref, sh0_ref, g_ref, mu_ref, win_ref, w0_ref, w1_ref, w2_ref, a0_ref, a1_ref, a2_ref,
                     ka_ref, rk_ref, seg_ref, bc_ref,
                     r_ref, w_ref, k_ref, v_ref, a_ref, gt_ref, bonus_ref, sho_ref, carry_ref, *, tm):
    ti = pl.program_id(1)

    @pl.when(ti == 0)
    def _():
        carry_ref[...] = sh0_ref[0]

    h = _rms(x_ref[0], g_ref[...])
    rows = lax.broadcasted_iota(jnp.int32, h.shape, 0)
    h_prev = jnp.where(rows == 0, carry_ref[...], pltpu.roll(h, 1, 0))
    carry_ref[...] = h[tm - 1:tm, :]
    sho_ref[0] = h[tm - 1:tm, :]
    xx = h_prev - h

    def mix(s):
        return (h + xx * mu_ref[s:s + 1, :]).astype(BF16)

    r = jnp.dot(mix(0), win_ref[:, 0:D_MODEL], preferred_element_type=F32)
    k = jnp.dot(mix(1), win_ref[:, D_MODEL:2 * D_MODEL], preferred_element_type=F32)
    v = jnp.dot(mix(2), win_ref[:, 2 * D_MODEL:3 * D_MODEL], preferred_element_type=F32)
    gt_ref[0] = jnp.dot(mix(3), win_ref[:, 3 * D_MODEL:4 * D_MODEL], preferred_element_type=F32)
    r_ref[0] = r
    v_ref[0] = v

    wl = jnp.tanh(jnp.dot(mix(4), w1_ref[...], preferred_element_type=F32))
    wx = w0_ref[...] + _dot(wl, w2_ref[...])
    w_log = -(jnp.maximum(-wx, 0.0) + jnp.log1p(jnp.exp(-jnp.abs(wx)))) - 0.5
    w_ref[0] = jnp.exp(-jnp.exp(w_log))

    al = jnp.dot(mix(5), a1_ref[...], preferred_element_type=F32)
    a = jax.nn.sigmoid(a0_ref[...] + _dot(al, a2_ref[...]))

    k_ref[0] = k
    a_ref[0] = a
    kh = k * (1.0 + (a - 1.0) * ka_ref[...])
    rk = _segsum(r * kh * rk_ref[...], seg_ref[...])
    bonus_ref[0] = _segbcast(rk, bc_ref[...]) * v


def _layer_c1(x, sh0, g, mu, w_in, w0, w1, w2, a0, a1, a2, k_a, r_k, *, tm):
    B, T, D = x.shape
    seg, bc = _seg_mats()
    row = lambda v: v.reshape(1, -1)
    tile = pl.BlockSpec((1, tm, D), lambda b, t: (b, t, 0))
    outs = pl.pallas_call(
        functools.partial(_layer_c1_kernel, tm=tm),
        grid=(B, T // tm),
        in_specs=[
            tile,
            pl.BlockSpec((1, 1, D), lambda b, t: (b, 0, 0)),
            _const_spec((1, D)),
            _const_spec((6, D)),
            _const_spec((D, 4 * D)),
            _const_spec((1, D)),
            _const_spec((D, D_LORA)),
            _const_spec((D_LORA, D)),
            _const_spec((1, D)),
            _const_spec((D, D_LORA)),
            _const_spec((D_LORA, D)),
            _const_spec((1, D)),
            _const_spec((1, D)),
            _const_spec((D, LANES)),
            _const_spec((LANES, D)),
        ],
        out_specs=[tile] * 7 + [pl.BlockSpec((1, 1, D), lambda b, t: (b, 0, 0))],
        out_shape=[jax.ShapeDtypeStruct((B, T, D), F32)] * 7 + [jax.ShapeDtypeStruct((B, 1, D), F32)],
        scratch_shapes=[pltpu.VMEM((1, D), F32)],
        compiler_params=pltpu.CompilerParams(
            dimension_semantics=("arbitrary", "arbitrary"), vmem_limit_bytes=VMEM_LIMIT),
        name="layer_c1",
    )(x, sh0.reshape(B, 1, D), row(g), mu, w_in, row(w0), w1, w2, row(a0), a1, a2, row(k_a), row(r_k), seg, bc)
    return outs[:7], outs[7].reshape(B, D)


def _layer_c3_kernel(x_ref, o_ref, bonus_ref, gt_ref, gng_ref, gnb_ref, wout_ref, xo_ref):
    o = o_ref[0] * gng_ref[...] + gnb_ref[...]
    y = (o + bonus_ref[0]) * _silu(gt_ref[0])
    xo_ref[0] = x_ref[0] + _dot(y, wout_ref[...])


def _layer_c3(x, o, bonus, gt, gn_g, gn_b, w_out, *, tm):
    B, T, D = x.shape
    row = lambda v: v.reshape(1, -1)
    tile = pl.BlockSpec((1, tm, D), lambda b, t: (b, t, 0))
    return pl.pallas_call(
        _layer_c3_kernel,
        grid=(B, T // tm),
        in_specs=[tile, tile, tile, tile, _const_spec((1, D)), _const_spec((1, D)), _const_spec((D, D))],
        out_specs=tile,
        out_shape=jax.ShapeDtypeStruct((B, T, D), F32),
        compiler_params=pltpu.CompilerParams(
            dimension_semantics=("arbitrary", "arbitrary"), vmem_limit_bytes=VMEM_LIMIT),
        name="layer_c3",
    )(x, o, bonus, gt, row(gn_g), row(gn_b), w_out)


def _wkv_kernel(r_ref, w_ref, k_ref, a_ref, v_ref, kkt_ref, kat_ref, s0_ref, y_ref, so_ref,
                col_ref, vt_ref, yt_ref, *, tq, nb, packed):
    @pl.when(pl.program_id(0) == 0)
    def _():
        so_ref[...] = s0_ref[...]

    half = LANES // 2
    npar = 2 if packed else 1
    nrow = RWKV_HEAD // npar
    low = lax.broadcasted_iota(jnp.int32, (RWKV_HEAD, LANES), 1) < half
    low_r = lax.broadcasted_iota(jnp.int32, (nrow, LANES), 1) < half
    Q_R, Q_W, Q_K, Q_A, Q_B, Q_V = range(6)
    nblk = tq // SUBLANES

    def relayout(blk):
        t8 = pl.multiple_of(blk * SUBLANES, SUBLANES)
        for q, ref in ((Q_R, r_ref), (Q_W, w_ref), (Q_K, k_ref), (Q_A, a_ref), (Q_V, v_ref)):
            slabs = [ref[b, pl.ds(t8, SUBLANES), :].reshape(SUBLANES, RWKV_HEADS, RWKV_HEAD) for b in range(nb)]
            for g in range(SUBLANES // npar):
                mt = jnp.concatenate([slabs[b][npar * g + par] for par in range(npar) for b in range(nb)], axis=0).T
                t0 = t8 + npar * g
                if not packed:
                    if q == Q_V:
                        vt_ref[t0] = mt
                    else:
                        col_ref[t0, q] = mt
                elif q == Q_V:
                    top, bot = mt[0:nrow], mt[nrow:RWKV_HEAD]
                    vt_ref[t0] = jnp.where(low_r, top, pltpu.roll(bot, half, 1))
                    vt_ref[t0 + 1] = jnp.where(low_r, pltpu.roll(top, half, 1), bot)
                else:
                    rolled = pltpu.roll(mt, half, 1)
                    col_ref[t0, q] = jnp.where(low, mt, rolled)
                    col_ref[t0 + 1, q] = jnp.where(low, rolled, mt)
        for i in range(SUBLANES):
            t = t8 + i
            k, a = col_ref[t, Q_K], col_ref[t, Q_A]
            kk = k * kkt_ref[...]
            n2 = jnp.sum(kk * kk, axis=0, keepdims=True)
            kk = kk * jnp.minimum(lax.rsqrt(n2), 1e12)
            col_ref[t, Q_K] = k * (1.0 + (a - 1.0) * kat_ref[...])
            col_ref[t, Q_A] = -kk
            col_ref[t, Q_B] = kk * a

    def head_norm(o):
        d = o - jnp.mean(o, axis=0, keepdims=True)
        var = jnp.mean(d * d, axis=0, keepdims=True)
        return d * lax.rsqrt(var + RWKV_GN_EPS)

    def writeback(blk):
        t8 = pl.multiple_of(blk * SUBLANES, SUBLANES)
        slabs = [[None] * SUBLANES for _ in range(nb)]
        for g in range(SUBLANES // npar):
            t0 = t8 + npar * g
            if packed:
                y0, y1 = yt_ref[t0], yt_ref[t0 + 1]
                yc = jnp.concatenate([jnp.where(low_r, y0, pltpu.roll(y1, half, 1)),
                                      jnp.where(low_r, pltpu.roll(y0, half, 1), y1)], axis=0)
            else:
                yc = yt_ref[t0]
            yt = head_norm(yc).T
            for par in range(npar):
                for b in range(nb):
                    r0 = (par * nb + b) * RWKV_HEADS
                    slabs[b][npar * g + par] = yt[r0:r0 + RWKV_HEADS, :]
        for b in range(nb):
            y_ref[b, pl.ds(t8, SUBLANES), :] = jnp.stack(slabs[b], axis=0).reshape(SUBLANES, D_MODEL)

    def token(t, sa):
        tn = jnp.minimum(t + 1, tq - 1)
        vt = vt_ref[t]
        y, nxt = [None, None], [None, None]
        for j in range(RWKV_HEAD):
            row = lambda q: col_ref[t, q, j:j + 1, :]
            s = so_ref[j] * row(Q_W) + sa * row(Q_B) + vt * row(Q_K)
            so_ref[j] = s
            yj, nj = s * row(Q_R), s * col_ref[tn, Q_A, j:j + 1, :]
            y[j % 2] = yj if y[j % 2] is None else y[j % 2] + yj
            nxt[j % 2] = nj if nxt[j % 2] is None else nxt[j % 2] + nj
        yt_ref[t] = y[0] + y[1]
        return nxt[0] + nxt[1]

    def relayout_blocks(u, carry):
        for n in range(2):
            relayout(2 * u + n)
        return carry

    def writeback_blocks(u, carry):
        for n in range(2):
            writeback(2 * u + n)
        return carry

    lax.fori_loop(0, nblk // 2, relayout_blocks, 0)

    acc = [so_ref[0] * col_ref[0, Q_A, 0:1, :], so_ref[1] * col_ref[0, Q_A, 1:2, :]]
    for j in range(2, RWKV_HEAD):
        acc[j % 2] = acc[j % 2] + so_ref[j] * col_ref[0, Q_A, j:j + 1, :]
    lax.fori_loop(0, tq, token, acc[0] + acc[1])

    lax.fori_loop(0, nblk // 2, writeback_blocks, 0)


def _wkv_layer(r, w, k, v, a, k_k, k_a, s0):
    B, T, D = r.shape
    H, N = RWKV_HEADS, RWKV_HEAD
    packed = B * H == LANES // 2
    assert packed or B * H == LANES
    if packed:
        ni = N // 2
        st_in = s0.reshape(B, H, 2, ni, N).transpose(4, 3, 2, 0, 1).reshape(N, ni, LANES)
        st_out = lambda s: s.reshape(N, ni, 2, B, H).transpose(3, 4, 2, 1, 0).reshape(B, H, N, N)
    else:
        ni = N
        st_in = s0.transpose(3, 2, 0, 1).reshape(N, N, LANES)
        st_out = lambda s: s.reshape(N, N, B, H).transpose(2, 3, 1, 0)
    tq = min(T, 64)
    tok_spec = pl.BlockSpec((B, tq, D), lambda t: (0, t, 0))
    st_spec = pl.BlockSpec((N, ni, LANES), lambda t: (0, 0, 0))
    par_spec = pl.BlockSpec((N, LANES), lambda t: (0, 0))
    lane_tile = lambda p: jnp.tile(p.reshape(H, N).T, (1, LANES // H))
    y, s = pl.pallas_call(
        functools.partial(_wkv_kernel, tq=tq, nb=B, packed=packed),
        grid=(T // tq,),
        in_specs=[tok_spec] * 5 + [par_spec, par_spec, st_spec],
        out_specs=[tok_spec, st_spec],
        out_shape=[jax.ShapeDtypeStruct((B, T, D), F32), jax.ShapeDtypeStruct((N, ni, LANES), F32)],
        scratch_shapes=[pltpu.VMEM((tq, 5, N, LANES), F32), pltpu.VMEM((tq, ni, LANES), F32),
                        pltpu.VMEM((tq, ni, LANES), F32)],
        compiler_params=pltpu.CompilerParams(dimension_semantics=("arbitrary",), vmem_limit_bytes=VMEM_LIMIT),
        name="wkv",
    )(r, w, k, a, v, lane_tile(k_k), lane_tile(k_a), st_in)
    return y, st_out(s)


def kernel(x_prompt, x_sample, cache_conv_a, cache_conv_b, state_lru_b, state_shift_c, state_wkv_c, state_ret_d, meta_tokens, norm_g, final_norm_g, a_w_in, a_w_dw, a_b_dw, a_ln_g, a_ln_b, a_w_out, b_w_in, b_w_conv, b_b_conv, b_w_rg, b_b_rg, b_w_ig, b_b_ig, b_lam, b_w_out, c_mu, c_w_in, c_w0, c_w1, c_w2, c_a0, c_a1, c_a2, c_k_k, c_k_a, c_r_k, c_gn_g, c_gn_b, c_w_out, d_w_in, d_gn_g, d_w_out):
    bf = lambda z: z.astype(BF16)
    a_win, a_wout = bf(a_w_in), bf(a_w_out)
    b_win, b_wout = bf(b_w_in), bf(b_w_out)
    b_wg = bf(jnp.concatenate([b_w_rg, b_w_ig], axis=-1))
    c_win, c_wout = bf(c_w_in), bf(c_w_out)
    c_w1b, c_w2b, c_a1b, c_a2b = bf(c_w1), bf(c_w2), bf(c_a1), bf(c_a2)
    d_win, d_wout = bf(d_w_in), bf(d_w_out)

    def trunk(x, st, pos0, tm):
        tm_ab = 2 * tm if x.shape[1] % (2 * tm) == 0 and tm >= 256 else tm
        x, conv_a = _layer_a(x, st[0], norm_g[0], a_win, a_w_dw, a_b_dw, a_ln_g, a_ln_b, a_wout, tm=tm_ab)
        x, conv_b, lru = _layer_b(x, st[1], st[2], norm_g[1], b_win, b_w_conv, b_b_conv, b_wg, b_b_rg, b_b_ig,
                                  b_lam, b_wout, tm=tm_ab)
        (r, w, k, v, a, gt, bonus), shift = _layer_c1(
            x, st[3], norm_g[2], c_mu, c_win, c_w0, c_w1b, c_w2b, c_a0, c_a1b, c_a2b, c_k_a,
            c_r_k.reshape(-1), tm=tm)
        o, wkv = _wkv_layer(r, w, k, v, a, c_k_k, c_k_a, st[4])
        x = _layer_c3(x, o, bonus, gt, c_gn_g, c_gn_b, c_wout, tm=tm)
        x, ret = _layer_d(x, st[5], pos0, norm_g[3], d_win, d_gn_g, d_wout, final_norm_g, tm=tm)
        return x, (conv_a, conv_b, lru, shift, wkv, ret)

    B = x_prompt.shape[0]
    zeros = (jnp.zeros((B, CONV_W - 1, D_MODEL), F32), jnp.zeros((B, LRU_CONV_W - 1, D_RNN), F32),
             jnp.zeros((B, D_RNN), F32), jnp.zeros((B, D_MODEL), F32),
             jnp.zeros((B, RWKV_HEADS, RWKV_HEAD, RWKV_HEAD), F32), jnp.zeros((B, RET_HEADS, RET_DK, RET_DV), F32))
    meta = jnp.broadcast_to(meta_tokens[None], (B, N_META, D_MODEL))
    _, st_meta = trunk(meta, zeros, 0, N_META)
    y_prompt, stp = trunk(x_prompt, st_meta, N_META, 256)
    st_s = (cache_conv_a, cache_conv_b, state_lru_b, state_shift_c, state_wkv_c, state_ret_d)
    y_sample, sts = trunk(x_sample, st_s, N_META + PAST_LEN, x_sample.shape[1])
    return (y_prompt, y_sample, stp[0], sts[0], stp[1], sts[1], stp[2], sts[2], stp[3], sts[3],
            stp[4], sts[4], stp[5], sts[5])
```
